```python
import math
import jax, jax.numpy as jnp
from jax import lax
import numpy as np

D_MODEL = 1024
BATCH = 16
SEQ = 2048
DEPTH = 2
DEC_BATCH = 32
DEC_SEQ = 32
PAST_LEN = 1024

CHUNK = 64
PLE_DIM = 256
EPS = 1e-6
GM_CHUNK = 128
GM_GROUPS = 8
GM_DIM = D_MODEL
GM_GDIM = GM_DIM // GM_GROUPS
SSM_HEADS = 16
SSM_HEAD_DIM = 64
SSM_DIM = SSM_HEADS * SSM_HEAD_DIM
SSM_GROUPS = 2
SSM_STATE = 128
SSM_CONV = 4
SSM_CONV_DIM = SSM_DIM + 2 * SSM_GROUPS * SSM_STATE
RET_HEADS = 8
RET_DK = 128
RET_DV = 256
RET_QK = RET_HEADS * RET_DK
RET_V = RET_HEADS * RET_DV
ROPE_BASE = 10000.0
D_FF = 2816
FFN_CONV = 3
IN0 = 2 * GM_DIM + SSM_DIM + SSM_CONV_DIM + SSM_HEADS
MIX0 = GM_DIM + SSM_DIM
IN1 = 2 * RET_QK + 2 * RET_V
SPLIT0 = (GM_DIM, 2 * GM_DIM, 2 * GM_DIM + SSM_DIM, 2 * GM_DIM + SSM_DIM + SSM_CONV_DIM)
SPLIT1 = (RET_QK, 2 * RET_QK, 2 * RET_QK + RET_V)

kernel_name = 'chunk_causal_gmlp_ssd_retention_convffn_step'


def rmsnorm(x, g):
    xf = x.astype(jnp.float32)
    y = xf * lax.rsqrt(jnp.mean(xf * xf, axis=-1, keepdims=True) + EPS)
    return (y * g.astype(jnp.float32)).astype(x.dtype)


def causal_dwconv(x, buf, w, b):
    width = w.shape[0]
    length = x.shape[1]
    xp = jnp.concatenate([buf.astype(x.dtype), x], axis=1)
    y = b.astype(x.dtype)
    for k in range(width):
        y = y + w[k] * xp[:, k:k + length]
    return y, xp[:, length:]


def gmlp_mix(u, v, v_gain, w_s, b_s):
    bsz, length, _ = u.shape
    n = min(length, GM_CHUNK)
    u = jax.nn.gelu(u)
    v = rmsnorm(jax.nn.gelu(v).reshape(bsz, length, GM_GROUPS, GM_GDIM), v_gain.reshape(GM_GROUPS, GM_GDIM))
    pos = jnp.arange(n)
    mask = (pos[None, :] // CHUNK) <= (pos[:, None] // CHUNK)
    w = jnp.where(mask[None], w_s[:, :n, :n], 0.0).astype(v.dtype)
    vc = v.reshape(bsz, length // n, n, GM_GROUPS, GM_GDIM)
    s = jnp.einsum('gij,bcjgd->bcigd', w, vc) + b_s[:, :n].T[None, None, :, :, None].astype(v.dtype)
    return u * s.reshape(bsz, length, GM_DIM), v.reshape(bsz, length, GM_DIM)


def ssd_scan(x, dt, a, bm, cm, h0):
    f32 = jnp.float32
    bsz, length = x.shape[:2]
    t_len = min(length, CHUNK)
    nc = length // t_len
    r = SSM_HEADS // SSM_GROUPS
    xs = (x * dt[..., None]).astype(f32).reshape(bsz, nc, t_len, SSM_GROUPS, r, SSM_HEAD_DIM)
    da = (dt.astype(f32) * a.astype(f32)).reshape(bsz, nc, t_len, SSM_GROUPS, r)
    bc = bm.astype(f32).reshape(bsz, nc, t_len, SSM_GROUPS, SSM_STATE)
    cc = cm.astype(f32).reshape(bsz, nc, t_len, SSM_GROUPS, SSM_STATE)
    cs = jnp.cumsum(da, axis=2)
    tril = jnp.tril(jnp.ones((t_len, t_len), bool))[None, None, :, :, None, None]
    seg = cs[:, :, :, None] - cs[:, :, None, :]
    lmat = jnp.exp(jnp.where(tril, seg, -jnp.inf))
    cb = jnp.einsum('bctgn,bcsgn->bctsg', cc, bc)
    y_diag = jnp.einsum('bctsg,bctsgr,bcsgrp->bctgrp', cb, lmat, xs)
    decay_to_end = jnp.exp(cs[:, :, -1:] - cs)
    chunk_states = jnp.einsum('bctgn,bctgr,bctgrp->bcgrpn', bc, decay_to_end, xs)
    chunk_decay = jnp.exp(cs[:, :, -1])

    def step(h, inp):
        st, dec = inp
        return h * dec[..., None, None] + st, h

    h_init = h0.astype(f32).reshape(bsz, SSM_GROUPS, r, SSM_HEAD_DIM, SSM_STATE)
    h_last, h_prev = lax.scan(step, h_init, (jnp.moveaxis(chunk_states, 1, 0), jnp.moveaxis(chunk_decay, 1, 0)))
    h_prev = jnp.moveaxis(h_prev, 0, 1)
    y_off = jnp.einsum('bctgn,bcgrpn,bctgr->bctgrp', cc, h_prev, jnp.exp(cs))
    y = (y_diag + y_off).reshape(bsz, length, SSM_HEADS, SSM_HEAD_DIM)
    return y, h_last.reshape(bsz, SSM_HEADS, SSM_HEAD_DIM, SSM_STATE).astype(h0.dtype)


def mamba_mix(z, xbc, dt_raw, conv_buf, h0, conv_w, conv_b, dt_bias, a_log, d_skip, norm_g):
    bsz, length = z.shape[:2]
    xbc, new_buf = causal_dwconv(xbc, conv_buf, conv_w, conv_b)
    xbc = jax.nn.silu(xbc)
    xh, bm, cm = jnp.split(xbc, [SSM_DIM, SSM_DIM + SSM_GROUPS * SSM_STATE], axis=-1)
    xh = xh.reshape(bsz, length, SSM_HEADS, SSM_HEAD_DIM)
    bm = bm.reshape(bsz, length, SSM_GROUPS, SSM_STATE)
    cm = cm.reshape(bsz, length, SSM_GROUPS, SSM_STATE)
    dt = jax.nn.softplus((dt_raw + dt_bias).astype(jnp.float32))
    a = -jnp.exp(a_log.astype(jnp.float32))
    y, h_last = ssd_scan(xh, dt, a, bm, cm, h0)
    y = (y + d_skip.astype(jnp.float32)[:, None] * xh.astype(jnp.float32)).reshape(bsz, length, SSM_DIM).astype(z.dtype)
    y = rmsnorm(y * jax.nn.silu(z), norm_g)
    return y, new_buf, h_last


def rope(x, pos):
    half = x.shape[-1] // 2
    inv = ROPE_BASE ** (-jnp.arange(half, dtype=jnp.float32) / half)
    ang = pos.astype(jnp.float32)[:, None] * inv[None]
    cos = jnp.cos(ang)[None, :, None, :]
    sin = jnp.sin(ang)[None, :, None, :]
    xf = x.astype(jnp.float32)
    x1, x2 = xf[..., :half], xf[..., half:]
    return jnp.concatenate([x1 * cos - x2 * sin, x1 * sin + x2 * cos], axis=-1)


def retention(q, k, v, s0):
    f32 = jnp.float32
    bsz, length = q.shape[:2]
    t_len = min(length, CHUNK)
    nc = length // t_len
    lg = jnp.log(1.0 - 2.0 ** (-5.0 - jnp.arange(RET_HEADS, dtype=f32)))
    qc = q.reshape(bsz, nc, t_len, RET_HEADS, RET_DK)
    kc = k.reshape(bsz, nc, t_len, RET_HEADS, RET_DK)
    vc = v.astype(f32).reshape(bsz, nc, t_len, RET_HEADS, RET_DV)
    t = jnp.arange(t_len, dtype=f32)
    diff = t[:, None] - t[None, :]
    dmat = jnp.where(diff[None] >= 0, jnp.exp(lg[:, None, None] * jnp.maximum(diff, 0.0)[None]), 0.0)
    scores = jnp.einsum('bcthd,bcshd->bchts', qc, kc) * dmat[None, None]
    y_intra = jnp.einsum('bchts,bcshe->bcthe', scores, vc)
    q_decay = jnp.exp(lg[None, :] * (t[:, None] + 1.0))
    k_decay = jnp.exp(lg[None, :] * (t_len - 1.0 - t)[:, None])
    chunk_kv = jnp.einsum('bcthd,th,bcthe->bchde', kc, k_decay, vc)
    chunk_decay = jnp.exp(lg * t_len)

    def step(s, kv):
        return s * chunk_decay[None, :, None, None] + kv, s

    s_last, s_prev = lax.scan(step, s0.astype(f32), jnp.moveaxis(chunk_kv, 1, 0))
    s_prev = jnp.moveaxis(s_prev, 0, 1)
    y_inter = jnp.einsum('bcthd,bchde,th->bcthe', qc, s_prev, q_decay)
    y = (y_intra + y_inter).reshape(bsz, length, RET_HEADS, RET_DV)
    return y, s_last.astype(s0.dtype)


def head_norm(y, g):
    mu = jnp.mean(y, axis=-1, keepdims=True)
    var = jnp.mean(jnp.square(y - mu), axis=-1, keepdims=True)
    yn = (y - mu) * lax.rsqrt(var + EPS)
    return yn.reshape(y.shape[0], y.shape[1], RET_V) * g.astype(jnp.float32)


def retention_mix(h, s0, pos, w_in, w_out, norm_g):
    bsz, length = h.shape[:2]
    proj = h @ w_in
    q, k, v, g = jnp.split(proj, SPLIT1, axis=-1)
    q = rope(q.reshape(bsz, length, RET_HEADS, RET_DK), pos)
    k = rope(k.reshape(bsz, length, RET_HEADS, RET_DK), pos) * (RET_DK ** -0.5)
    v = v.reshape(bsz, length, RET_HEADS, RET_DV)
    y, s_last = retention(q, k, v, s0)
    y = head_norm(y, norm_g).astype(h.dtype)
    return (jax.nn.silu(g) * y) @ w_out, s_last


def conv_ffn(h, buf, w_gate, w_up, conv_w, conv_b, w_down):
    g = h @ w_gate
    u = h @ w_up
    g, new_buf = causal_dwconv(g, buf, conv_w, conv_b)
    return (jax.nn.silu(g) * u) @ w_down, new_buf


def trunk(x, p, ssm_conv_buf, ssm_h, ret_s, ffn_bufs, pos0, weights):
    (norm_mix, norm_ffn, norm_ple, norm_final, l0_w_in, gm_v_gain, gm_w_s, gm_b_s,
     ssm_conv_w, ssm_conv_b, ssm_dt_bias, ssm_a_log, ssm_d, ssm_norm, l0_w_out,
     l1_w_in, ret_norm, l1_w_out, ffn_w_gate, ffn_w_up, ffn_conv_w, ffn_conv_b,
     ffn_w_down, ple_w_gate, ple_w_proj) = weights
    length = x.shape[1]
    pos = pos0 + jnp.arange(length)
    new_ffn = []
    for i in range(DEPTH):
        h = rmsnorm(x, norm_mix[i])
        if i % 2 == 0:
            proj = h @ l0_w_in
            u, v, z, xbc, dt_raw = jnp.split(proj, SPLIT0, axis=-1)
            ya, gm_v = gmlp_mix(u, v, gm_v_gain, gm_w_s, gm_b_s)
            yb, ssm_conv_buf, ssm_h = mamba_mix(z, xbc, dt_raw, ssm_conv_buf, ssm_h, ssm_conv_w, ssm_conv_b,
                                                ssm_dt_bias, ssm_a_log, ssm_d, ssm_norm)
            x = x + jnp.concatenate([ya, yb], axis=-1) @ l0_w_out
        else:
            yc, ret_s = retention_mix(h, ret_s, pos, l1_w_in, l1_w_out, ret_norm)
            x = x + yc
        h = rmsnorm(x, norm_ffn[i])
        yf, fb = conv_ffn(h, ffn_bufs[i], ffn_w_gate[i], ffn_w_up[i], ffn_conv_w[i], ffn_conv_b[i], ffn_w_down[i])
        new_ffn.append(fb)
        x = x + yf
        gate = jax.nn.sigmoid(rmsnorm(x, norm_ple[i]) @ ple_w_gate[i])
        x = x + gate * (p[i] @ ple_w_proj[i])
    x = rmsnorm(x, norm_final)
    return x, gm_v, ssm_conv_buf, ssm_h, ret_s, jnp.stack(new_ffn)


def setup_inputs(seed: int = 0) -> dict:
    key = jax.random.key(seed)
    keys = iter(jax.random.split(key, 48))
    f32 = jnp.float32

    def nrm(shape, scale):
        return jax.random.normal(next(keys), shape, f32) * scale

    def gain(shape):
        return 1.0 + 0.02 * jax.random.normal(next(keys), shape, f32)

    dt0 = jnp.exp(jax.random.uniform(next(keys), (SSM_HEADS,), f32, math.log(1e-3), math.log(1e-1)))
    return {
        'x_prompt': nrm((BATCH, SEQ, D_MODEL), 1.0),
        'x_sample': nrm((DEC_BATCH, DEC_SEQ, D_MODEL), 1.0),
        'state_ssm_conv': nrm((DEC_BATCH, SSM_CONV - 1, SSM_CONV_DIM), 1.0),
        'state_ssm': nrm((DEC_BATCH, SSM_HEADS, SSM_HEAD_DIM, SSM_STATE), 0.1),
        'state_ret': nrm((DEC_BATCH, RET_HEADS, RET_DK, RET_DV), 0.1),
        'state_ffn_conv': nrm((DEPTH, DEC_BATCH, FFN_CONV - 1, D_FF), 1.0),
        'p_prompt': nrm((DEPTH, BATCH, SEQ, PLE_DIM), 1.0),
        'p_sample': nrm((DEPTH, DEC_BATCH, DEC_SEQ, PLE_DIM), 1.0),
        'norm_mix': gain((DEPTH, D_MODEL)),
        'norm_ffn': gain((DEPTH, D_MODEL)),
        'norm_ple': gain((DEPTH, D_MODEL)),
        'norm_final': gain((D_MODEL,)),
        'l0_w_in': nrm((D_MODEL, IN0), D_MODEL ** -0.5),
        'gm_v_gain': gain((GM_DIM,)),
        'gm_w_s': nrm((GM_GROUPS, GM_CHUNK, GM_CHUNK), GM_CHUNK ** -0.5),
        'gm_b_s': gain((GM_GROUPS, GM_CHUNK)),
        'ssm_conv_w': nrm((SSM_CONV, SSM_CONV_DIM), SSM_CONV ** -0.5),
        'ssm_conv_b': nrm((SSM_CONV_DIM,), 0.02),
        'ssm_dt_bias': dt0 + jnp.log(-jnp.expm1(-dt0)),
        'ssm_a_log': jnp.log(jax.random.uniform(next(keys), (SSM_HEADS,), f32, 1.0, 16.0)),
        'ssm_d': gain((SSM_HEADS,)),
        'ssm_norm': gain((SSM_DIM,)),
        'l0_w_out': nrm((MIX0, D_MODEL), MIX0 ** -0.5),
        'l1_w_in': nrm((D_MODEL, IN1), D_MODEL ** -0.5),
        'ret_norm': gain((RET_V,)),
        'l1_w_out': nrm((RET_V, D_MODEL), RET_V ** -0.5),
        'ffn_w_gate': nrm((DEPTH, D_MODEL, D_FF), D_MODEL ** -0.5),
        'ffn_w_up': nrm((DEPTH, D_MODEL, D_FF), D_MODEL ** -0.5),
        'ffn_conv_w': nrm((DEPTH, FFN_CONV, D_FF), FFN_CONV ** -0.5),
        'ffn_conv_b': nrm((DEPTH, D_FF), 0.02),
        'ffn_w_down': nrm((DEPTH, D_FF, D_MODEL), D_FF ** -0.5),
        'ple_w_gate': nrm((DEPTH, D_MODEL, D_MODEL), D_MODEL ** -0.5),
        'ple_w_proj': nrm((DEPTH, PLE_DIM, D_MODEL), PLE_DIM ** -0.5),
    }


def reference(x_prompt, x_sample, state_ssm_conv, state_ssm, state_ret, state_ffn_conv, p_prompt, p_sample,
              norm_mix, norm_ffn, norm_ple, norm_final, l0_w_in, gm_v_gain, gm_w_s, gm_b_s,
              ssm_conv_w, ssm_conv_b, ssm_dt_bias, ssm_a_log, ssm_d, ssm_norm, l0_w_out,
              l1_w_in, ret_norm, l1_w_out, ffn_w_gate, ffn_w_up, ffn_conv_w, ffn_conv_b,
              ffn_w_down, ple_w_gate, ple_w_proj):
    weights = (norm_mix, norm_ffn, norm_ple, norm_final, l0_w_in, gm_v_gain, gm_w_s, gm_b_s,
               ssm_conv_w, ssm_conv_b, ssm_dt_bias, ssm_a_log, ssm_d, ssm_norm, l0_w_out,
               l1_w_in, ret_norm, l1_w_out, ffn_w_gate, ffn_w_up, ffn_conv_w, ffn_conv_b,
               ffn_w_down, ple_w_gate, ple_w_proj)
    bp = x_prompt.shape[0]
    dtp = x_prompt.dtype
    zero_ssm_conv = jnp.zeros((bp, SSM_CONV - 1, SSM_CONV_DIM), dtp)
    zero_ssm = jnp.zeros((bp, SSM_HEADS, SSM_HEAD_DIM, SSM_STATE), state_ssm.dtype)
    zero_ret = jnp.zeros((bp, RET_HEADS, RET_DK, RET_DV), state_ret.dtype)
    zero_ffn = jnp.zeros((DEPTH, bp, FFN_CONV - 1, D_FF), dtp)
    y_prompt, _, ssm_conv_p, ssm_p, ret_p, ffn_p = trunk(
        x_prompt, p_prompt, zero_ssm_conv, zero_ssm, zero_ret, zero_ffn, 0, weights)
    y_sample, gm_v_s, ssm_conv_s, ssm_s, ret_s, ffn_s = trunk(
        x_sample, p_sample, state_ssm_conv, state_ssm, state_ret, state_ffn_conv, PAST_LEN, weights)
    return (y_prompt, y_sample, ssm_conv_p, ssm_conv_s, ssm_p, ssm_s, ret_p, ret_s, ffn_p, ffn_s, gm_v_s)
```

```python
import functools
import math

import jax
import jax.numpy as jnp
from jax import lax
from jax.experimental import pallas as pl
from jax.experimental.pallas import tpu as pltpu

F32 = jnp.float32
BF16 = jnp.bfloat16

D_MODEL = 1024
DEPTH = 2
PAST_LEN = 1024
CHUNK = 64
PLE_DIM = 256
EPS = 1e-6
GM_CHUNK = 128
GM_GROUPS = 8
GM_GDIM = 128
SSM_HEADS = 16
SSM_HEAD_DIM = 64
SSM_DIM = 1024
SSM_GROUPS = 2
SSM_STATE = 128
SSM_CONV = 4
SSM_CONV_DIM = 1536
RET_HEADS = 8
RET_DK = 128
RET_DV = 256
RET_QK = 1024
RET_V = 2048
ROPE_BASE = 10000.0
D_FF = 2816
FFN_CONV = 3

SUBLANES = 8
GROUP_LANES = SSM_HEADS // SSM_GROUPS * SSM_HEAD_DIM
VMEM_LIMIT = 56 * 1024 * 1024


def _dot(a, b):
    return jnp.dot(a, b, preferred_element_type=F32)


def _dot_nt(a, b):
    return lax.dot_general(a, b, (((1,), (1,)), ((), ())), preferred_element_type=F32)


def _dot_tn(a, b):
    return lax.dot_general(a, b, (((0,), (0,)), ((), ())), preferred_element_type=F32)


def _rms(x, g):
    return x * lax.rsqrt(jnp.mean(x * x, axis=-1, keepdims=True) + EPS) * g


def _silu(x):
    return x * jax.nn.sigmoid(x)


def _softplus(x):
    return jnp.maximum(x, 0.0) + jnp.log1p(jnp.exp(-jnp.abs(x)))


def _causal_conv(pad_ref, w_ref, b_ref, tl):
    width = w_ref.shape[0]
    y = b_ref[...][None]
    for k in range(width):
        start = SUBLANES - (width - 1) + k
        y = y + w_ref[k:k + 1, :][None] * pad_ref[:, start:start + tl, :]
    return y


def _mixer0_kernel(x_ref, sconv_ref, sssm_ref, nm_ref, wuvz_ref, wxbc_ref, wdt_ref, dtb_ref, alog_ref,
                   dskip_ref, cw_ref, cb_ref, snorm_ref, vgain_ref, gmw_ref, gmb_ref, wout_ref,
                   xo_ref, convo_ref, ht_ref, *rest, ns, tl, tc, n_gm, emit_gmv):
    if emit_gmv:
        gmv_ref, xpad_ref, xs_ref, cs_ref, y_ref, mix_ref = rest
    else:
        xpad_ref, xs_ref, cs_ref, y_ref, mix_ref = rest
    m = ns * tl
    nck = tl // tc
    nchunks = m // tc
    step = pl.program_id(1)

    @pl.when(step == 0)
    def _():
        xpad_ref[:, 0:SUBLANES, :] = sconv_ref[...]
        ht_ref[...] = sssm_ref[...]

    x = x_ref[...].reshape(m, D_MODEL)
    h = _rms(x, nm_ref[...]).astype(BF16)

    uvz = _dot(h, wuvz_ref[...])
    ri = lax.broadcasted_iota(jnp.int32, (n_gm, n_gm), 0)
    ci = lax.broadcasted_iota(jnp.int32, (n_gm, n_gm), 1)
    gm_mask = (ci // CHUNK) <= (ri // CHUNK)
    for g in range(GM_GROUPS):
        lanes = slice(g * GM_GDIM, (g + 1) * GM_GDIM)
        u_g = jax.nn.gelu(uvz[:, lanes])
        v_g = jax.nn.gelu(uvz[:, GM_GROUPS * GM_GDIM + g * GM_GDIM:GM_GROUPS * GM_GDIM + (g + 1) * GM_GDIM])
        v_g = v_g * lax.rsqrt(jnp.mean(v_g * v_g, axis=-1, keepdims=True) + EPS) * vgain_ref[:, lanes]
        if emit_gmv:
            gmv_ref[:, :, lanes] = v_g.reshape(ns, tl, GM_GDIM)
        w_g = jnp.where(gm_mask, gmw_ref[g], 0.0).astype(BF16)
        v_gb = v_g.astype(BF16)
        for c in range(m // n_gm):
            rows = slice(c * n_gm, (c + 1) * n_gm)
            s = _dot(w_g, v_gb[rows]) + gmb_ref[:, lanes]
            mix_ref[rows, lanes] = (u_g[rows] * s).astype(BF16)
    z = uvz[:, 2 * GM_GROUPS * GM_GDIM:]

    xpad_ref[:, SUBLANES:, :] = _dot(h, wxbc_ref[...]).reshape(ns, tl, SSM_CONV_DIM)
    xc = _silu(_causal_conv(xpad_ref, cw_ref, cb_ref, tl)).reshape(m, SSM_CONV_DIM)
    xpad_ref[:, 0:SUBLANES, :] = xpad_ref[:, tl:tl + SUBLANES, :]
    convo_ref[...] = xpad_ref[:, 0:SUBLANES, :]
    xh = xc[:, :SSM_DIM]
    bm = xc[:, SSM_DIM:SSM_DIM + SSM_GROUPS * SSM_STATE].astype(BF16)
    cm = xc[:, SSM_DIM + SSM_GROUPS * SSM_STATE:].astype(BF16)

    dtx = _softplus(_dot(h, wdt_ref[...]) + dtb_ref[...])
    da = dtx * (-jnp.exp(alog_ref[...]))
    xs_ref[...] = xh * dtx
    y_ref[...] = dskip_ref[...] * xh

    ri = lax.broadcasted_iota(jnp.int32, (m, m), 0)
    ci = lax.broadcasted_iota(jnp.int32, (m, m), 1)
    tri = jnp.where(((ri // tc) == (ci // tc)) & (ci <= ri), 1.0, 0.0).astype(BF16)
    da_hi = da.astype(BF16)
    da_lo = (da - da_hi.astype(F32)).astype(BF16)
    cs_ref[...] = _dot(tri, da_hi) + _dot(tri, da_lo)

    def dup_rows(b_g):
        parts = []
        for c in range(nchunks):
            blk = b_g[c * tc:(c + 1) * tc]
            if tc < CHUNK:
                blk = jnp.concatenate([blk, jnp.zeros((CHUNK - tc, SSM_STATE), BF16)], axis=0)
            parts += [blk, blk]
        return jnp.concatenate(parts, axis=0)

    cb2 = [_dot_nt(cm[:, g * SSM_STATE:(g + 1) * SSM_STATE], dup_rows(bm[:, g * SSM_STATE:(g + 1) * SSM_STATE]))
           for g in range(SSM_GROUPS)]

    t_i = lax.broadcasted_iota(jnp.int32, (tc, SSM_DIM), 0)
    s_i = lax.broadcasted_iota(jnp.int32, (tc, SSM_DIM), 1) % SSM_HEAD_DIM
    causal = s_i <= t_i
    diag = s_i == t_i
    quad_head = lax.broadcasted_iota(jnp.int32, (tc, 4 * SSM_HEAD_DIM), 1) // SSM_HEAD_DIM
    heads_per_group = SSM_HEADS // SSM_GROUPS

    for c in range(nchunks):
        seq = c // nck
        rows = slice(c * tc, (c + 1) * tc)
        cs_c = cs_ref[rows, :]
        cs_row = jnp.sum(jnp.where(diag, cs_c, 0.0), axis=0, keepdims=True)
        cs_last = cs_c[tc - 1:tc, :]
        lmat = jnp.exp(jnp.where(causal, cs_c - cs_row, -1e30))
        cb_c = jnp.concatenate(
            [cb2[g][rows, c * 2 * CHUNK:(c + 1) * 2 * CHUNK] for g in range(SSM_GROUPS)
             for _ in range(heads_per_group // 2)], axis=1)
        mm = (lmat * cb_c).astype(BF16)
        xs_c = xs_ref[rows, :]
        y_parts = []
        for q in range(SSM_HEADS // 4):
            ql = slice(q * 4 * SSM_HEAD_DIM, (q + 1) * 4 * SSM_HEAD_DIM)
            xq = xs_c[:, ql]
            blocks = []
            for hh in range(4):
                blk = jnp.where(quad_head == hh, xq, 0.0).astype(BF16)
                if tc < CHUNK:
                    blk = jnp.concatenate([blk, jnp.zeros((CHUNK - tc, 4 * SSM_HEAD_DIM), BF16)], axis=0)
                blocks.append(blk)
            y_parts.append(_dot(mm[:, ql], jnp.concatenate(blocks, axis=0)))
        y_c = jnp.concatenate(y_parts, axis=1)
        ecs = jnp.exp(cs_c)
        xsd = (xs_c * jnp.exp(cs_last - cs_c)).astype(BF16)
        e_last = ecs[tc - 1:tc, :]
        y_off = []
        for g in range(SSM_GROUPS):
            gl = slice(g * GROUP_LANES, (g + 1) * GROUP_LANES)
            sl = slice(g * SSM_STATE, (g + 1) * SSM_STATE)
            h_prev = ht_ref[seq, g]
            y_off.append(_dot(cm[rows, sl], h_prev.astype(BF16)))
            ht_ref[seq, g] = h_prev * e_last[:, gl] + _dot_tn(bm[rows, sl], xsd[:, gl])
        y_ref[rows, :] = y_ref[rows, :] + y_c + jnp.concatenate(y_off, axis=1) * ecs

    yb = _rms(y_ref[...] * _silu(z), snorm_ref[...])
    mix_ref[:, GM_GROUPS * GM_GDIM:] = yb.astype(BF16)
    xo_ref[...] = (x + _dot(mix_ref[...], wout_ref[...])).reshape(ns, tl, D_MODEL)


def _mixer1_kernel(x_ref, sret_ref, nm_ref, win_ref, cos_ref, sin_ref, dmat_ref, qdec_ref, kdec_ref,
                   cdec_ref, rnorm_ref, wout_ref, xo_ref, s_ref, mix_ref, *, ns, tl):
    m = ns * tl
    step = pl.program_id(1)

    @pl.when(step == 0)
    def _():
        s_ref[...] = sret_ref[...]

    x = x_ref[...].reshape(m, D_MODEL)
    h = _rms(x, nm_ref[...]).astype(BF16)
    cos = jnp.concatenate([cos_ref[...]] * ns, axis=0) if ns > 1 else cos_ref[...]
    sin = jnp.concatenate([sin_ref[...]] * ns, axis=0) if ns > 1 else sin_ref[...]
    qdec = qdec_ref[...]
    kdec = kdec_ref[...]
    scale = RET_DK ** -0.5

    qk = _dot(h, win_ref[:, :2 * RET_QK])
    for hd in range(RET_HEADS):
        kl = slice(hd * RET_DK, (hd + 1) * RET_DK)
        vl = slice(hd * RET_DV, (hd + 1) * RET_DV)
        q = qk[:, kl]
        k = qk[:, RET_QK + hd * RET_DK:RET_QK + (hd + 1) * RET_DK]
        v = _dot(h, win_ref[:, 2 * RET_QK + hd * RET_DV:2 * RET_QK + (hd + 1) * RET_DV]).astype(BF16)
        gate = _dot(h, win_ref[:, 2 * RET_QK + RET_V + hd * RET_DV:2 * RET_QK + RET_V + (hd + 1) * RET_DV])
        q = q * cos + pltpu.roll(q, RET_DK // 2, axis=1) * sin
        k = (k * cos + pltpu.roll(k, RET_DK // 2, axis=1) * sin) * scale
        sc = _dot_nt(q.astype(BF16), k.astype(BF16)) * dmat_ref[hd]
        y = _dot(sc.astype(BF16), v)
        qd = (q * qdec[:, kl]).astype(BF16)
        kd = (k * kdec[:, kl]).astype(BF16)
        y_inter = []
        for s in range(ns):
            rows = slice(s * tl, (s + 1) * tl)
            s_prev = s_ref[s, hd]
            y_inter.append(_dot(qd[rows], s_prev.astype(BF16)))
            s_ref[s, hd] = s_prev * cdec_ref[:, vl] + _dot_tn(kd[rows], v[rows])
        y = y + (jnp.concatenate(y_inter, axis=0) if ns > 1 else y_inter[0])
        mu = jnp.mean(y, axis=-1, keepdims=True)
        yc = y - mu
        var = jnp.mean(yc * yc, axis=-1, keepdims=True)
        yn = yc * lax.rsqrt(var + EPS) * rnorm_ref[:, vl]
        mix_ref[:, vl] = (_silu(gate) * yn).astype(BF16)

    xo_ref[...] = (x + _dot(mix_ref[...], wout_ref[...])).reshape(ns, tl, D_MODEL)


def _ffn_kernel(x_ref, p_ref, sbuf_ref, nf_ref, wg_ref, wu_ref, cw_ref, cb_ref, wd_ref, npl_ref, wpg_ref,
                wpp_ref, nfin_ref, xo_ref, bufo_ref, gpad_ref, *, ns, tl, final_norm):
    m = ns * tl
    step = pl.program_id(1)

    @pl.when(step == 0)
    def _():
        gpad_ref[:, 0:SUBLANES, :] = sbuf_ref[...]

    x = x_ref[...].reshape(m, D_MODEL)
    h = _rms(x, nf_ref[...]).astype(BF16)
    gpad_ref[:, SUBLANES:, :] = _dot(h, wg_ref[...]).reshape(ns, tl, D_FF)
    gate = _causal_conv(gpad_ref, cw_ref, cb_ref, tl).reshape(m, D_FF)
    gpad_ref[:, 0:SUBLANES, :] = gpad_ref[:, tl:tl + SUBLANES, :]
    bufo_ref[...] = gpad_ref[:, 0:SUBLANES, :]
    act = (_silu(gate) * _dot(h, wu_ref[...])).astype(BF16)
    x = x + _dot(act, wd_ref[...])
    pgate = jax.nn.sigmoid(_dot(_rms(x, npl_ref[...]).astype(BF16), wpg_ref[...]))
    x = x + pgate * _dot(p_ref[...].reshape(m, PLE_DIM).astype(BF16), wpp_ref[...])
    if final_norm:
        x = _rms(x, nfin_ref[...])
    xo_ref[...] = x.reshape(ns, tl, D_MODEL)


def _const_spec(shape):
    zeros = (0,) * len(shape)
    return pl.BlockSpec(shape, lambda b, l: zeros, pipeline_mode=pl.Buffered(1))


def _params():
    return pltpu.CompilerParams(dimension_semantics=("arbitrary", "arbitrary"), vmem_limit_bytes=VMEM_LIMIT)


def _tile_spec(ns, tl, width):
    return pl.BlockSpec((ns, tl, width), lambda b, l: (b, l, 0))


def _state_spec(ns, tail):
    zeros = (0,) * len(tail)
    return pl.BlockSpec((ns,) + tail, lambda b, l: (b,) + zeros)


def _mixer0(x, sconv, sssm, w, *, ns, tl, emit_gmv):
    bsz, length, _ = x.shape
    tc = min(tl, CHUNK)
    n_gm = min(tl, GM_CHUNK)
    m = ns * tl
    grid = (bsz // ns, length // tl)
    consts = [w['nm0'], w['wuvz'], w['wxbc'], w['wdt'], w['dtb'], w['alog'], w['dskip'], w['ssm_cw'], w['ssm_cb'],
              w['snorm'], w['vgain'], w['gmw'][:, :n_gm, :n_gm], w['gmb'][:n_gm], w['wout0']]
    out_shape = [jax.ShapeDtypeStruct(x.shape, F32),
                 jax.ShapeDtypeStruct((bsz, SUBLANES, SSM_CONV_DIM), F32),
                 jax.ShapeDtypeStruct(sssm.shape, F32)]
    out_specs = [_tile_spec(ns, tl, D_MODEL), _state_spec(ns, (SUBLANES, SSM_CONV_DIM)),
                 _state_spec(ns, sssm.shape[1:])]
    if emit_gmv:
        out_shape.append(jax.ShapeDtypeStruct((bsz, length, GM_GROUPS * GM_GDIM), F32))
        out_specs.append(_tile_spec(ns, tl, GM_GROUPS * GM_GDIM))
    return pl.pallas_call(
        functools.partial(_mixer0_kernel, ns=ns, tl=tl, tc=tc, n_gm=n_gm, emit_gmv=emit_gmv),
        grid=grid,
        in_specs=[_tile_spec(ns, tl, D_MODEL), _state_spec(ns, (SUBLANES, SSM_CONV_DIM)),
                  _state_spec(ns, sssm.shape[1:])] + [_const_spec(c.shape) for c in consts],
        out_specs=out_specs,
        out_shape=out_shape,
        scratch_shapes=[pltpu.VMEM((ns, SUBLANES + tl, SSM_CONV_DIM), F32),
                        pltpu.VMEM((m, SSM_DIM), F32), pltpu.VMEM((m, SSM_DIM), F32),
                        pltpu.VMEM((m, SSM_DIM), F32), pltpu.VMEM((m, 2 * SSM_DIM), BF16)],
        compiler_params=_params(),
        name="mixer0",
    )(x, sconv, sssm, *consts)


def _mixer1(x, sret, w, tabs, *, ns, tl):
    bsz, length, _ = x.shape
    m = ns * tl
    grid = (bsz // ns, length // tl)
    consts_a = [w['nm1'], w['win1']]
    consts_b = [tabs['dmat'], tabs['qdec'], tabs['kdec'], tabs['cdec'], w['rnorm'], w['wout1']]
    pos_spec = pl.BlockSpec((tl, RET_DK), lambda b, l: (l, 0))
    return pl.pallas_call(
        functools.partial(_mixer1_kernel, ns=ns, tl=tl),
        grid=grid,
        in_specs=[_tile_spec(ns, tl, D_MODEL), _state_spec(ns, sret.shape[1:])]
        + [_const_spec(c.shape) for c in consts_a] + [pos_spec, pos_spec]
        + [_const_spec(c.shape) for c in consts_b],
        out_specs=[_tile_spec(ns, tl, D_MODEL), _state_spec(ns, sret.shape[1:])],
        out_shape=[jax.ShapeDtypeStruct(x.shape, F32), jax.ShapeDtypeStruct(sret.shape, F32)],
        scratch_shapes=[pltpu.VMEM((m, RET_V), BF16)],
        compiler_params=_params(),
        name="mixer1",
    )(x, sret, *consts_a, tabs['cos'], tabs['sin'], *consts_b)


def _ffn(x, p, sbuf, w, layer, *, ns, tl, final_norm):
    bsz, length, _ = x.shape
    grid = (bsz // ns, length // tl)
    consts = [w['nf'][layer], w['wg'][layer], w['wu'][layer], w['ffn_cw'][layer], w['ffn_cb'][layer],
              w['wd'][layer], w['npl'][layer], w['wpg'][layer], w['wpp'][layer], w['nfin']]
    p_spec = pl.BlockSpec((None, ns, tl, PLE_DIM), lambda b, l: (layer, b, l, 0))
    return pl.pallas_call(
        functools.partial(_ffn_kernel, ns=ns, tl=tl, final_norm=final_norm),
        grid=grid,
        in_specs=[_tile_spec(ns, tl, D_MODEL), p_spec, _state_spec(ns, (SUBLANES, D_FF))]
        + [_const_spec(c.shape) for c in consts],
        out_specs=[_tile_spec(ns, tl, D_MODEL), _state_spec(ns, (SUBLANES, D_FF))],
        out_shape=[jax.ShapeDtypeStruct(x.shape, F32), jax.ShapeDtypeStruct((bsz, SUBLANES, D_FF), F32)],
        scratch_shapes=[pltpu.VMEM((ns, SUBLANES + tl, D_FF), F32)],
        compiler_params=_params(),
        name="ffn%d" % layer,
    )(x, p, sbuf, *consts)


def _retention_tables(ns, tl, pos0, length):
    half = RET_DK // 2
    inv = ROPE_BASE ** (-jnp.arange(half, dtype=F32) / half)
    ang = (pos0 + jnp.arange(length)).astype(F32)[:, None] * inv[None]
    cos, sin = jnp.cos(ang), jnp.sin(ang)
    lg = jnp.log(1.0 - 2.0 ** (-5.0 - jnp.arange(RET_HEADS, dtype=F32)))
    m = ns * tl
    t = jnp.arange(m) % tl
    same_seq = (jnp.arange(m)[:, None] // tl) == (jnp.arange(m)[None, :] // tl)
    diff = (t[:, None] - t[None, :]).astype(F32)
    dmat = jnp.where((diff[None] >= 0) & same_seq[None],
                     jnp.exp(lg[:, None, None] * jnp.maximum(diff, 0.0)[None]), 0.0)
    tf = t.astype(F32)
    qdec = jnp.exp(lg[None, :] * (tf[:, None] + 1.0))
    kdec = jnp.exp(lg[None, :] * (tl - 1.0 - tf)[:, None])
    cdec = jnp.exp(lg * tl)
    return {
        'cos': jnp.concatenate([cos, cos], axis=-1),
        'sin': jnp.concatenate([-sin, sin], axis=-1),
        'dmat': dmat,
        'qdec': jnp.repeat(qdec, RET_DK, axis=1),
        'kdec': jnp.repeat(kdec, RET_DK, axis=1),
        'cdec': jnp.repeat(cdec, RET_DV)[None],
    }


def _header(buf):
    return jnp.pad(buf, ((0, 0), (SUBLANES - buf.shape[1], 0), (0, 0)))


def _trunk(x, p, sconv, sssm, sret, sffn, pos0, w, *, ns, ns_ret, tl, emit_gmv):
    bsz, length, _ = x.shape
    hpg = SSM_HEADS // SSM_GROUPS
    sssm_t = sssm.reshape(bsz, SSM_GROUPS, hpg, SSM_HEAD_DIM, SSM_STATE).transpose(0, 1, 4, 2, 3)
    sssm_t = sssm_t.reshape(bsz, SSM_GROUPS, SSM_STATE, GROUP_LANES)
    outs = _mixer0(x, _header(sconv), sssm_t, w, ns=ns, tl=tl, emit_gmv=emit_gmv)
    x, conv_o, ssm_o = outs[:3]
    gmv = outs[3] if emit_gmv else None
    x, ffn0 = _ffn(x, p, _header(sffn[0]), w, 0, ns=ns, tl=tl, final_norm=False)
    tabs = _retention_tables(ns_ret, tl, pos0, length)
    x, ret_o = _mixer1(x, sret, w, tabs, ns=ns_ret, tl=tl)
    x, ffn1 = _ffn(x, p, _header(sffn[1]), w, 1, ns=ns, tl=tl, final_norm=True)
    ssm_o = ssm_o.reshape(bsz, SSM_GROUPS, SSM_STATE, hpg, SSM_HEAD_DIM).transpose(0, 1, 3, 4, 2)
    ssm_o = ssm_o.reshape(bsz, SSM_HEADS, SSM_HEAD_DIM, SSM_STATE)
    new_ffn = jnp.stack([ffn0[:, SUBLANES - (FFN_CONV - 1):], ffn1[:, SUBLANES - (FFN_CONV - 1):]])
    return x, gmv, conv_o[:, SUBLANES - (SSM_CONV - 1):], ssm_o, ret_o, new_ffn


def _prepare_weights(norm_mix, norm_ffn, norm_ple, norm_final, l0_w_in, gm_v_gain, gm_w_s, gm_b_s,
                     ssm_conv_w, ssm_conv_b, ssm_dt_bias, ssm_a_log, ssm_d, ssm_norm, l0_w_out,
                     l1_w_in, ret_norm, l1_w_out, ffn_w_gate, ffn_w_up, ffn_conv_w, ffn_conv_b,
                     ffn_w_down, ple_w_gate, ple_w_proj):
    gm = GM_GROUPS * GM_GDIM
    rep = lambda v: jnp.repeat(v, SSM_HEAD_DIM)[None].astype(F32)
    return {
        'nm0': norm_mix[0][None], 'nm1': norm_mix[1][None],
        'wuvz': l0_w_in[:, :2 * gm + SSM_DIM].astype(BF16),
        'wxbc': l0_w_in[:, 2 * gm + SSM_DIM:2 * gm + SSM_DIM + SSM_CONV_DIM].astype(BF16),
        'wdt': jnp.repeat(l0_w_in[:, 2 * gm + SSM_DIM + SSM_CONV_DIM:], SSM_HEAD_DIM, axis=1).astype(BF16),
        'dtb': rep(ssm_dt_bias), 'alog': rep(ssm_a_log), 'dskip': rep(ssm_d),
        'ssm_cw': ssm_conv_w, 'ssm_cb': ssm_conv_b[None], 'snorm': ssm_norm[None], 'vgain': gm_v_gain[None],
        'gmw': gm_w_s, 'gmb': jnp.repeat(gm_b_s.T, GM_GDIM, axis=1),
        'wout0': l0_w_out.astype(BF16),
        'win1': l1_w_in.astype(BF16), 'rnorm': ret_norm[None], 'wout1': l1_w_out.astype(BF16),
        'nf': norm_ffn[:, None], 'wg': ffn_w_gate.astype(BF16), 'wu': ffn_w_up.astype(BF16),
        'ffn_cw': ffn_conv_w, 'ffn_cb': ffn_conv_b[:, None], 'wd': ffn_w_down.astype(BF16),
        'npl': norm_ple[:, None], 'wpg': ple_w_gate.astype(BF16), 'wpp': ple_w_proj.astype(BF16),
        'nfin': norm_final[None],
    }


def kernel(x_prompt, x_sample, state_ssm_conv, state_ssm, state_ret, state_ffn_conv, p_prompt, p_sample, norm_mix, norm_ffn, norm_ple, norm_final, l0_w_in, gm_v_gain, gm_w_s, gm_b_s, ssm_conv_w, ssm_conv_b, ssm_dt_bias, ssm_a_log, ssm_d, ssm_norm, l0_w_out, l1_w_in, ret_norm, l1_w_out, ffn_w_gate, ffn_w_up, ffn_conv_w, ffn_conv_b, ffn_w_down, ple_w_gate, ple_w_proj):
    w = _prepare_weights(norm_mix, norm_ffn, norm_ple, norm_final, l0_w_in, gm_v_gain, gm_w_s, gm_b_s,
                         ssm_conv_w, ssm_conv_b, ssm_dt_bias, ssm_a_log, ssm_d, ssm_norm, l0_w_out,
                         l1_w_in, ret_norm, l1_w_out, ffn_w_gate, ffn_w_up, ffn_conv_w, ffn_conv_b,
                         ffn_w_down, ple_w_gate, ple_w_proj)
    bp = x_prompt.shape[0]
    zero_conv = jnp.zeros((bp, SSM_CONV - 1, SSM_CONV_DIM), F32)
    zero_ssm = jnp.zeros((bp, SSM_HEADS, SSM_HEAD_DIM, SSM_STATE), F32)
    zero_ret = jnp.zeros((bp, RET_HEADS, RET_DK, RET_DV), F32)
    zero_ffn = jnp.zeros((DEPTH, bp, FFN_CONV - 1, D_FF), F32)
    y_p, _, conv_p, ssm_p, ret_p, ffn_p = _trunk(
        x_prompt, p_prompt, zero_conv, zero_ssm, zero_ret, zero_ffn, 0, w, ns=1, ns_ret=1, tl=256, emit_gmv=False)
    y_s, gmv_s, conv_s, ssm_s, ret_s, ffn_s = _trunk(
        x_sample, p_sample, state_ssm_conv, state_ssm, state_ret, state_ffn_conv, PAST_LEN, w,
        ns=8, ns_ret=4, tl=32, emit_gmv=True)
    return (y_p, y_s, conv_p, conv_s, ssm_p, ssm_s, ret_p, ret_s, ffn_p, ffn_s, gmv_s)
```

```python
import functools
import math

import jax
import jax.numpy as jnp
from jax import lax
from jax.experimental import pallas as pl
from jax.experimental.pallas import tpu as pltpu

F32 = jnp.float32
BF16 = jnp.bfloat16

D_MODEL = 1024
DEPTH = 2
PAST_LEN = 1024
CHUNK = 64
PLE_DIM = 256
EPS = 1e-6
GM_CHUNK = 128
GM_GROUPS = 8
GM_GDIM = 128
SSM_HEADS = 16
SSM_HEAD_DIM = 64
SSM_DIM = 1024
SSM_GROUPS = 2
SSM_STATE = 128
SSM_CONV = 4
SSM_CONV_DIM = 1536
RET_HEADS = 8
RET_DK = 128
RET_DV = 256
RET_QK = 1024
RET_V = 2048
ROPE_BASE = 10000.0
D_FF = 2816
FFN_CONV = 3

SUBLANES = 8
LANES = 128
GROUP_LANES = SSM_HEADS // SSM_GROUPS * SSM_HEAD_DIM
DT_LANES = 128
VMEM_LIMIT = 56 * 1024 * 1024


def _dot(a, b):
    return jnp.dot(a, b, preferred_element_type=F32)


def _dot_nt(a, b):
    return lax.dot_general(a, b, (((1,), (1,)), ((), ())), preferred_element_type=F32)


def _dot_tn(a, b):
    return lax.dot_general(a, b, (((0,), (0,)), ((), ())), preferred_element_type=F32)


def _rms(x, g):
    return x * lax.rsqrt(jnp.mean(x * x, axis=-1, keepdims=True) + EPS) * g


def _silu(x):
    return x * jax.nn.sigmoid(x)


def _softplus(x):
    return jnp.maximum(x, 0.0) + jnp.log1p(jnp.exp(-jnp.abs(x)))


def _pad_store(pad_ref, start, value):
    rows = value.shape[1]
    for j in range(pad_ref.shape[1]):
        pad_ref[:, j, start:start + rows, :] = value[:, :, j * LANES:(j + 1) * LANES]


def _pad_load(pad_ref, start, rows):
    return jnp.concatenate([pad_ref[:, j, start:start + rows, :] for j in range(pad_ref.shape[1])], axis=-1)


def _causal_conv(pad_ref, w_ref, b_ref, tl):
    width = w_ref.shape[0]
    y = b_ref[...][None]
    for k in range(width):
        y = y + w_ref[k:k + 1, :][None] * _pad_load(pad_ref, SUBLANES - (width - 1) + k, tl)
    return y


def _mixer0_kernel(x_ref, sconv_ref, sssm_ref, nm_ref, wuvz_ref, wxbc_ref, wdt_ref, dtb_ref, alog_ref, expd_ref,
                   dskip_ref, cw_ref, cb_ref, snorm_ref, vgain_ref, gmw_ref, gmb_ref, wout_ref,
                   xo_ref, convo_ref, ht_ref, *rest, ns, tl, tc, n_gm, emit_gmv):
    if emit_gmv:
        gmv_ref, xpad_ref, xs_ref, cs_ref, y_ref, mix_ref = rest
    else:
        xpad_ref, xs_ref, cs_ref, y_ref, mix_ref = rest
    m = ns * tl
    nck = tl // tc
    nchunks = m // tc
    step = pl.program_id(1)

    @pl.when(step == 0)
    def _():
        _pad_store(xpad_ref, 0, sconv_ref[...])
        ht_ref[...] = sssm_ref[...]

    x = x_ref[...].reshape(m, D_MODEL)
    h = _rms(x, nm_ref[...]).astype(BF16)

    uvz = _dot(h, wuvz_ref[...])
    ri = lax.broadcasted_iota(jnp.int32, (n_gm, n_gm), 0)
    ci = lax.broadcasted_iota(jnp.int32, (n_gm, n_gm), 1)
    gm_mask = (ci // CHUNK) <= (ri // CHUNK)
    for g in range(GM_GROUPS):
        lanes = slice(g * GM_GDIM, (g + 1) * GM_GDIM)
        u_g = jax.nn.gelu(uvz[:, lanes])
        v_g = jax.nn.gelu(uvz[:, GM_GROUPS * GM_GDIM + g * GM_GDIM:GM_GROUPS * GM_GDIM + (g + 1) * GM_GDIM])
        v_g = v_g * lax.rsqrt(jnp.mean(v_g * v_g, axis=-1, keepdims=True) + EPS) * vgain_ref[:, lanes]
        if emit_gmv:
            gmv_ref[:, :, lanes] = v_g.reshape(ns, tl, GM_GDIM)
        w_g = jnp.where(gm_mask, gmw_ref[g], 0.0).astype(BF16)
        v_gb = v_g.astype(BF16)
        for c in range(m // n_gm):
            rows = slice(c * n_gm, (c + 1) * n_gm)
            s = _dot(w_g, v_gb[rows]) + gmb_ref[:, lanes]
            mix_ref[rows, lanes] = (u_g[rows] * s).astype(BF16)
    z = uvz[:, 2 * GM_GROUPS * GM_GDIM:]

    _pad_store(xpad_ref, SUBLANES, _dot(h, wxbc_ref[...]).reshape(ns, tl, SSM_CONV_DIM))
    xc = _silu(_causal_conv(xpad_ref, cw_ref, cb_ref, tl)).reshape(m, SSM_CONV_DIM)
    tail = _pad_load(xpad_ref, tl, SUBLANES)
    _pad_store(xpad_ref, 0, tail)
    convo_ref[...] = tail
    xh = xc[:, :SSM_DIM]
    bm = xc[:, SSM_DIM:SSM_DIM + SSM_GROUPS * SSM_STATE].astype(BF16)
    cm = xc[:, SSM_DIM + SSM_GROUPS * SSM_STATE:].astype(BF16)

    dt_n = _softplus(_dot(h, wdt_ref[...]) + dtb_ref[...])
    da_n = dt_n * (-jnp.exp(alog_ref[...]))

    def split3(v):
        hi = v.astype(BF16).astype(F32)
        r1 = v - hi
        mid = r1.astype(BF16).astype(F32)
        return hi, mid, r1 - mid

    lane_grp = lax.broadcasted_iota(jnp.int32, (m, DT_LANES), 1) // SSM_HEADS

    def expand(v):
        hi, mid, lo = split3(v)
        parts = jnp.where(lane_grp == 0, hi, jnp.where(lane_grp == 1, mid, lo))
        return _dot(parts.astype(BF16), expd_ref[...])

    ri = lax.broadcasted_iota(jnp.int32, (m, m), 0)
    ci = lax.broadcasted_iota(jnp.int32, (m, m), 1)
    tri = jnp.where(((ri // tc) == (ci // tc)) & (ci <= ri), 1.0, 0.0).astype(BF16)
    cs3 = _dot(tri, jnp.concatenate(split3(da_n), axis=1).astype(BF16))
    cs_n = cs3[:, :DT_LANES] + cs3[:, DT_LANES:2 * DT_LANES] + cs3[:, 2 * DT_LANES:]
    cs_ref[...] = expand(cs_n)
    dtx = expand(dt_n)
    xs_ref[...] = xh * dtx
    y_ref[...] = dskip_ref[...] * xh

    def dup_rows(b_g):
        parts = []
        for c in range(nchunks):
            blk = b_g[c * tc:(c + 1) * tc]
            if tc < CHUNK:
                blk = jnp.concatenate([blk, jnp.zeros((CHUNK - tc, SSM_STATE), BF16)], axis=0)
            parts += [blk, blk]
        return jnp.concatenate(parts, axis=0)

    cb2 = [_dot_nt(cm[:, g * SSM_STATE:(g + 1) * SSM_STATE], dup_rows(bm[:, g * SSM_STATE:(g + 1) * SSM_STATE]))
           for g in range(SSM_GROUPS)]

    t_i = lax.broadcasted_iota(jnp.int32, (tc, SSM_DIM), 0)
    s_i = lax.broadcasted_iota(jnp.int32, (tc, SSM_DIM), 1) % SSM_HEAD_DIM
    causal = s_i <= t_i
    diag = s_i == t_i
    quad_head = lax.broadcasted_iota(jnp.int32, (tc, 4 * SSM_HEAD_DIM), 1) // SSM_HEAD_DIM
    heads_per_group = SSM_HEADS // SSM_GROUPS

    for c in range(nchunks):
        seq = c // nck
        rows = slice(c * tc, (c + 1) * tc)
        cs_c = cs_ref[rows, :]
        cs_row = jnp.sum(jnp.where(diag, cs_c, 0.0), axis=0, keepdims=True)
        cs_last = cs_c[tc - 1:tc, :]
        lmat = jnp.exp(jnp.where(causal, cs_c - cs_row, -1e30))
        cb_c = jnp.concatenate(
            [cb2[g][rows, c * 2 * CHUNK:(c + 1) * 2 * CHUNK] for g in range(SSM_GROUPS)
             for _ in range(heads_per_group // 2)], axis=1)
        mm = (lmat * cb_c).astype(BF16)
        xs_c = xs_ref[rows, :]
        y_parts = []
        for q in range(SSM_HEADS // 4):
            ql = slice(q * 4 * SSM_HEAD_DIM, (q + 1) * 4 * SSM_HEAD_DIM)
            xq = xs_c[:, ql]
            blocks = []
            for hh in range(4):
                blk = jnp.where(quad_head == hh, xq, 0.0).astype(BF16)
                if tc < CHUNK:
                    blk = jnp.concatenate([blk, jnp.zeros((CHUNK - tc, 4 * SSM_HEAD_DIM), BF16)], axis=0)
                blocks.append(blk)
            y_parts.append(_dot(mm[:, ql], jnp.concatenate(blocks, axis=0)))
        y_c = jnp.concatenate(y_parts, axis=1)
        ecs = jnp.exp(cs_c)
        xsd = (xs_c * jnp.exp(cs_last - cs_c)).astype(BF16)
        e_last = ecs[tc - 1:tc, :]
        y_off = []
        for g in range(SSM_GROUPS):
            gl = slice(g * GROUP_LANES, (g + 1) * GROUP_LANES)
            sl = slice(g * SSM_STATE, (g + 1) * SSM_STATE)
            h_prev = ht_ref[seq, g]
            y_off.append(_dot(cm[rows, sl], h_prev.astype(BF16)))
            ht_ref[seq, g] = h_prev * e_last[:, gl] + _dot_tn(bm[rows, sl], xsd[:, gl])
        y_ref[rows, :] = y_ref[rows, :] + y_c + jnp.concatenate(y_off, axis=1) * ecs

    yb = _rms(y_ref[...] * _silu(z), snorm_ref[...])
    mix_ref[:, GM_GROUPS * GM_GDIM:] = yb.astype(BF16)
    xo_ref[...] = (x + _dot(mix_ref[...], wout_ref[...])).reshape(ns, tl, D_MODEL)


def _mixer1_kernel(x_ref, sret_ref, nm_ref, win_ref, cos_ref, sin_ref, dmat_ref, qdec_ref, kdec_ref,
                   cdec_ref, rnorm_ref, wout_ref, xo_ref, s_ref, mix_ref, *, ns, tl):
    m = ns * tl
    step = pl.program_id(1)

    @pl.when(step == 0)
    def _():
        s_ref[...] = sret_ref[...]

    x = x_ref[...].reshape(m, D_MODEL)
    h = _rms(x, nm_ref[...]).astype(BF16)
    cos = jnp.concatenate([cos_ref[...]] * ns, axis=0) if ns > 1 else cos_ref[...]
    sin = jnp.concatenate([sin_ref[...]] * ns, axis=0) if ns > 1 else sin_ref[...]
    qdec = qdec_ref[...]
    kdec = kdec_ref[...]
    scale = RET_DK ** -0.5

    qk = _dot(h, win_ref[:, :2 * RET_QK])
    for hd in range(RET_HEADS):
        kl = slice(hd * RET_DK, (hd + 1) * RET_DK)
        vl = slice(hd * RET_DV, (hd + 1) * RET_DV)
        q = qk[:, kl]
        k = qk[:, RET_QK + hd * RET_DK:RET_QK + (hd + 1) * RET_DK]
        v = _dot(h, win_ref[:, 2 * RET_QK + hd * RET_DV:2 * RET_QK + (hd + 1) * RET_DV]).astype(BF16)
        gate = _dot(h, win_ref[:, 2 * RET_QK + RET_V + hd * RET_DV:2 * RET_QK + RET_V + (hd + 1) * RET_DV])
        q = q * cos + pltpu.roll(q, RET_DK // 2, axis=1) * sin
        k = (k * cos + pltpu.roll(k, RET_DK // 2, axis=1) * sin) * scale
        sc = _dot_nt(q.astype(BF16), k.astype(BF16)) * dmat_ref[hd]
        y = _dot(sc.astype(BF16), v)
        qd = (q * qdec[:, kl]).astype(BF16)
        kd = (k * kdec[:, kl]).astype(BF16)
        y_inter = []
        for s in range(ns):
            rows = slice(s * tl, (s + 1) * tl)
            s_prev = s_ref[s, hd]
            y_inter.append(_dot(qd[rows], s_prev.astype(BF16)))
            s_ref[s, hd] = s_prev * cdec_ref[:, vl] + _dot_tn(kd[rows], v[rows])
        y = y + (jnp.concatenate(y_inter, axis=0) if ns > 1 else y_inter[0])
        mu = jnp.mean(y, axis=-1, keepdims=True)
        yc = y - mu
        var = jnp.mean(yc * yc, axis=-1, keepdims=True)
        yn = yc * lax.rsqrt(var + EPS) * rnorm_ref[:, vl]
        mix_ref[:, vl] = (_silu(gate) * yn).astype(BF16)

    xo_ref[...] = (x + _dot(mix_ref[...], wout_ref[...])).reshape(ns, tl, D_MODEL)


def _ffn_kernel(x_ref, p_ref, sbuf_ref, nf_ref, wg_ref, wu_ref, cw_ref, cb_ref, wd_ref, npl_ref, wpg_ref,
                wpp_ref, nfin_ref, xo_ref, bufo_ref, gpad_ref, *, ns, tl, final_norm):
    m = ns * tl
    step = pl.program_id(1)

    @pl.when(step == 0)
    def _():
        _pad_store(gpad_ref, 0, sbuf_ref[...])

    x = x_ref[...].reshape(m, D_MODEL)
    h = _rms(x, nf_ref[...]).astype(BF16)
    _pad_store(gpad_ref, SUBLANES, _dot(h, wg_ref[...]).reshape(ns, tl, D_FF))
    gate = _causal_conv(gpad_ref, cw_ref, cb_ref, tl).reshape(m, D_FF)
    tail = _pad_load(gpad_ref, tl, SUBLANES)
    _pad_store(gpad_ref, 0, tail)
    bufo_ref[...] = tail
    act = (_silu(gate) * _dot(h, wu_ref[...])).astype(BF16)
    x = x + _dot(act, wd_ref[...])
    pgate = jax.nn.sigmoid(_dot(_rms(x, npl_ref[...]).astype(BF16), wpg_ref[...]))
    x = x + pgate * _dot(p_ref[...].reshape(m, PLE_DIM).astype(BF16), wpp_ref[...])
    if final_norm:
        x = _rms(x, nfin_ref[...])
    xo_ref[...] = x.reshape(ns, tl, D_MODEL)


def _const_spec(shape):
    zeros = (0,) * len(shape)
    return pl.BlockSpec(shape, lambda b, l: zeros, pipeline_mode=pl.Buffered(1))


def _params():
    return pltpu.CompilerParams(dimension_semantics=("arbitrary", "arbitrary"), vmem_limit_bytes=VMEM_LIMIT)


def _tile_spec(ns, tl, width):
    return pl.BlockSpec((ns, tl, width), lambda b, l: (b, l, 0))


def _state_spec(ns, tail):
    zeros = (0,) * len(tail)
    return pl.BlockSpec((ns,) + tail, lambda b, l: (b,) + zeros)


def _mixer0(x, sconv, sssm, w, *, ns, tl, emit_gmv):
    bsz, length, _ = x.shape
    tc = min(tl, CHUNK)
    n_gm = min(tl, GM_CHUNK)
    m = ns * tl
    grid = (bsz // ns, length // tl)
    consts = [w['nm0'], w['wuvz'], w['wxbc'], w['wdt'], w['dtb'], w['alog'], w['expd'], w['dskip'], w['ssm_cw'], w['ssm_cb'],
              w['snorm'], w['vgain'], w['gmw'][:, :n_gm, :n_gm], w['gmb'][:n_gm], w['wout0']]
    out_shape = [jax.ShapeDtypeStruct(x.shape, F32),
                 jax.ShapeDtypeStruct((bsz, SUBLANES, SSM_CONV_DIM), F32),
                 jax.ShapeDtypeStruct(sssm.shape, F32)]
    out_specs = [_tile_spec(ns, tl, D_MODEL), _state_spec(ns, (SUBLANES, SSM_CONV_DIM)),
                 _state_spec(ns, sssm.shape[1:])]
    if emit_gmv:
        out_shape.append(jax.ShapeDtypeStruct((bsz, length, GM_GROUPS * GM_GDIM), F32))
        out_specs.append(_tile_spec(ns, tl, GM_GROUPS * GM_GDIM))
    return pl.pallas_call(
        functools.partial(_mixer0_kernel, ns=ns, tl=tl, tc=tc, n_gm=n_gm, emit_gmv=emit_gmv),
        grid=grid,
        in_specs=[_tile_spec(ns, tl, D_MODEL), _state_spec(ns, (SUBLANES, SSM_CONV_DIM)),
                  _state_spec(ns, sssm.shape[1:])] + [_const_spec(c.shape) for c in consts],
        out_specs=out_specs,
        out_shape=out_shape,
        scratch_shapes=[pltpu.VMEM((ns, SSM_CONV_DIM // LANES, SUBLANES + tl, LANES), F32),
                        pltpu.VMEM((m, SSM_DIM), F32), pltpu.VMEM((m, SSM_DIM), F32),
                        pltpu.VMEM((m, SSM_DIM), F32), pltpu.VMEM((m, 2 * SSM_DIM), BF16)],
        compiler_params=_params(),
        name="mixer0",
    )(x, sconv, sssm, *consts)


def _mixer1(x, sret, w, tabs, *, ns, tl):
    bsz, length, _ = x.shape
    m = ns * tl
    grid = (bsz // ns, length // tl)
    consts_a = [w['nm1'], w['win1']]
    consts_b = [tabs['dmat'], tabs['qdec'], tabs['kdec'], tabs['cdec'], w['rnorm'], w['wout1']]
    pos_spec = pl.BlockSpec((tl, RET_DK), lambda b, l: (l, 0))
    return pl.pallas_call(
        functools.partial(_mixer1_kernel, ns=ns, tl=tl),
        grid=grid,
        in_specs=[_tile_spec(ns, tl, D_MODEL), _state_spec(ns, sret.shape[1:])]
        + [_const_spec(c.shape) for c in consts_a] + [pos_spec, pos_spec]
        + [_const_spec(c.shape) for c in consts_b],
        out_specs=[_tile_spec(ns, tl, D_MODEL), _state_spec(ns, sret.shape[1:])],
        out_shape=[jax.ShapeDtypeStruct(x.shape, F32), jax.ShapeDtypeStruct(sret.shape, F32)],
        scratch_shapes=[pltpu.VMEM((m, RET_V), BF16)],
        compiler_params=_params(),
        name="mixer1",
    )(x, sret, *consts_a, tabs['cos'], tabs['sin'], *consts_b)


def _ffn(x, p, sbuf, w, layer, *, ns, tl, final_norm):
    bsz, length, _ = x.shape
    grid = (bsz // ns, length // tl)
    consts = [w['nf'][layer], w['wg'][layer], w['wu'][layer], w['ffn_cw'][layer], w['ffn_cb'][layer],
              w['wd'][layer], w['npl'][layer], w['wpg'][layer], w['wpp'][layer], w['nfin']]
    p_spec = pl.BlockSpec((None, ns, tl, PLE_DIM), lambda b, l: (layer, b, l, 0))
    return pl.pallas_call(
        functools.partial(_ffn_kernel, ns=ns, tl=tl, final_norm=final_norm),
        grid=grid,
        in_specs=[_tile_spec(ns, tl, D_MODEL), p_spec, _state_spec(ns, (SUBLANES, D_FF))]
        + [_const_spec(c.shape) for c in consts],
        out_specs=[_tile_spec(ns, tl, D_MODEL), _state_spec(ns, (SUBLANES, D_FF))],
        out_shape=[jax.ShapeDtypeStruct(x.shape, F32), jax.ShapeDtypeStruct((bsz, SUBLANES, D_FF), F32)],
        scratch_shapes=[pltpu.VMEM((ns, D_FF // LANES, SUBLANES + tl, LANES), F32)],
        compiler_params=_params(),
        name="ffn%d" % layer,
    )(x, p, sbuf, *consts)


def _retention_tables(ns, tl, pos0, length):
    half = RET_DK // 2
    inv = ROPE_BASE ** (-jnp.arange(half, dtype=F32) / half)
    ang = (pos0 + jnp.arange(length)).astype(F32)[:, None] * inv[None]
    cos, sin = jnp.cos(ang), jnp.sin(ang)
    lg = jnp.log(1.0 - 2.0 ** (-5.0 - jnp.arange(RET_HEADS, dtype=F32)))
    m = ns * tl
    t = jnp.arange(m) % tl
    same_seq = (jnp.arange(m)[:, None] // tl) == (jnp.arange(m)[None, :] // tl)
    diff = (t[:, None] - t[None, :]).astype(F32)
    dmat = jnp.where((diff[None] >= 0) & same_seq[None],
                     jnp.exp(lg[:, None, None] * jnp.maximum(diff, 0.0)[None]), 0.0)
    tf = t.astype(F32)
    qdec = jnp.exp(lg[None, :] * (tf[:, None] + 1.0))
    kdec = jnp.exp(lg[None, :] * (tl - 1.0 - tf)[:, None])
    cdec = jnp.exp(lg * tl)
    return {
        'cos': jnp.concatenate([cos, cos], axis=-1),
        'sin': jnp.concatenate([-sin, sin], axis=-1),
        'dmat': dmat,
        'qdec': jnp.repeat(qdec, RET_DK, axis=1),
        'kdec': jnp.repeat(kdec, RET_DK, axis=1),
        'cdec': jnp.repeat(cdec, RET_DV)[None],
    }


def _header(buf):
    return jnp.pad(buf, ((0, 0), (SUBLANES - buf.shape[1], 0), (0, 0)))


def _trunk(x, p, sconv, sssm, sret, sffn, pos0, w, *, ns, ns_ret, tl, emit_gmv):
    bsz, length, _ = x.shape
    hpg = SSM_HEADS // SSM_GROUPS
    sssm_t = sssm.reshape(bsz, SSM_GROUPS, hpg, SSM_HEAD_DIM, SSM_STATE).transpose(0, 1, 4, 2, 3)
    sssm_t = sssm_t.reshape(bsz, SSM_GROUPS, SSM_STATE, GROUP_LANES)
    outs = _mixer0(x, _header(sconv), sssm_t, w, ns=ns, tl=tl, emit_gmv=emit_gmv)
    x, conv_o, ssm_o = outs[:3]
    gmv = outs[3] if emit_gmv else None
    x, ffn0 = _ffn(x, p, _header(sffn[0]), w, 0, ns=ns, tl=tl, final_norm=False)
    tabs = _retention_tables(ns_ret, tl, pos0, length)
    x, ret_o = _mixer1(x, sret, w, tabs, ns=ns_ret, tl=tl)
    x, ffn1 = _ffn(x, p, _header(sffn[1]), w, 1, ns=ns, tl=tl, final_norm=True)
    ssm_o = ssm_o.reshape(bsz, SSM_GROUPS, SSM_STATE, hpg, SSM_HEAD_DIM).transpose(0, 1, 3, 4, 2)
    ssm_o = ssm_o.reshape(bsz, SSM_HEADS, SSM_HEAD_DIM, SSM_STATE)
    new_ffn = jnp.stack([ffn0[:, SUBLANES - (FFN_CONV - 1):], ffn1[:, SUBLANES - (FFN_CONV - 1):]])
    return x, gmv, conv_o[:, SUBLANES - (SSM_CONV - 1):], ssm_o, ret_o, new_ffn


def _prepare_weights(norm_mix, norm_ffn, norm_ple, norm_final, l0_w_in, gm_v_gain, gm_w_s, gm_b_s,
                     ssm_conv_w, ssm_conv_b, ssm_dt_bias, ssm_a_log, ssm_d, ssm_norm, l0_w_out,
                     l1_w_in, ret_norm, l1_w_out, ffn_w_gate, ffn_w_up, ffn_conv_w, ffn_conv_b,
                     ffn_w_down, ple_w_gate, ple_w_proj):
    gm = GM_GROUPS * GM_GDIM
    rep = lambda v: jnp.repeat(v, SSM_HEAD_DIM)[None].astype(F32)
    tile = lambda v: jnp.tile(v, DT_LANES // SSM_HEADS)[None].astype(F32)
    lane = jnp.arange(DT_LANES)
    return {
        'nm0': norm_mix[0][None], 'nm1': norm_mix[1][None],
        'wuvz': l0_w_in[:, :2 * gm + SSM_DIM].astype(BF16),
        'wxbc': l0_w_in[:, 2 * gm + SSM_DIM:2 * gm + SSM_DIM + SSM_CONV_DIM].astype(BF16),
        'wdt': jnp.tile(l0_w_in[:, 2 * gm + SSM_DIM + SSM_CONV_DIM:], (1, DT_LANES // SSM_HEADS)).astype(BF16),
        'dtb': tile(ssm_dt_bias), 'alog': tile(ssm_a_log), 'dskip': rep(ssm_d),
        'expd': ((lane[:, None] < 3 * SSM_HEADS)
                 & (lane[:, None] % SSM_HEADS == jnp.arange(SSM_DIM)[None] // SSM_HEAD_DIM)).astype(BF16),
        'ssm_cw': ssm_conv_w, 'ssm_cb': ssm_conv_b[None], 'snorm': ssm_norm[None], 'vgain': gm_v_gain[None],
        'gmw': gm_w_s, 'gmb': jnp.repeat(gm_b_s.T, GM_GDIM, axis=1),
        'wout0': l0_w_out.astype(BF16),
        'win1': l1_w_in.astype(BF16), 'rnorm': ret_norm[None], 'wout1': l1_w_out.astype(BF16),
        'nf': norm_ffn[:, None], 'wg': ffn_w_gate.astype(BF16), 'wu': ffn_w_up.astype(BF16),
        'ffn_cw': ffn_conv_w, 'ffn_cb': ffn_conv_b[:, None], 'wd': ffn_w_down.astype(BF16),
        'npl': norm_ple[:, None], 'wpg': ple_w_gate.astype(BF16), 'wpp': ple_w_proj.astype(BF16),
        'nfin': norm_final[None],
    }


def kernel(x_prompt, x_sample, state_ssm_conv, state_ssm, state_ret, state_ffn_conv, p_prompt, p_sample, norm_mix, norm_ffn, norm_ple, norm_final, l0_w_in, gm_v_gain, gm_w_s, gm_b_s, ssm_conv_w, ssm_conv_b, ssm_dt_bias, ssm_a_log, ssm_d, ssm_norm, l0_w_out, l1_w_in, ret_norm, l1_w_out, ffn_w_gate, ffn_w_up, ffn_conv_w, ffn_conv_b, ffn_w_down, ple_w_gate, ple_w_proj):
    w = _prepare_weights(norm_mix, norm_ffn, norm_ple, norm_final, l0_w_in, gm_v_gain, gm_w_s, gm_b_s,
                         ssm_conv_w, ssm_conv_b, ssm_dt_bias, ssm_a_log, ssm_d, ssm_norm, l0_w_out,
                         l1_w_in, ret_norm, l1_w_out, ffn_w_gate, ffn_w_up, ffn_conv_w, ffn_conv_b,
                         ffn_w_down, ple_w_gate, ple_w_proj)
    bp = x_prompt.shape[0]
    zero_conv = jnp.zeros((bp, SSM_CONV - 1, SSM_CONV_DIM), F32)
    zero_ssm = jnp.zeros((bp, SSM_HEADS, SSM_HEAD_DIM, SSM_STATE), F32)
    zero_ret = jnp.zeros((bp, RET_HEADS, RET_DK, RET_DV), F32)
    zero_ffn = jnp.zeros((DEPTH, bp, FFN_CONV - 1, D_FF), F32)
    y_p, _, conv_p, ssm_p, ret_p, ffn_p = _trunk(
        x_prompt, p_prompt, zero_conv, zero_ssm, zero_ret, zero_ffn, 0, w, ns=1, ns_ret=1, tl=256, emit_gmv=False)
    y_s, gmv_s, conv_s, ssm_s, ret_s, ffn_s = _trunk(
        x_sample, p_sample, state_ssm_conv, state_ssm, state_ret, state_ffn_conv, PAST_LEN, w,
        ns=8, ns_ret=4, tl=32, emit_gmv=True)
    return (y_p, y_s, conv_p, conv_s, ssm_p, ssm_s, ret_p, ret_s, ffn_p, ffn_s, gmv_s)
```

```python
import functools

import jax
import jax.numpy as jnp
from jax import lax
from jax.experimental import pallas as pl
from jax.experimental.pallas import tpu as pltpu

F32 = jnp.float32
BF16 = jnp.bfloat16

D_MODEL = 1024
DEPTH = 2
PAST_LEN = 1024
CHUNK = 64
PLE_DIM = 256
EPS = 1e-6
GM_CHUNK = 128
GM_GROUPS = 8
GM_GDIM = 128
SSM_HEADS = 16
SSM_HEAD_DIM = 64
SSM_DIM = 1024
SSM_GROUPS = 2
SSM_STATE = 128
SSM_CONV = 4
SSM_CONV_DIM = 1536
RET_HEADS = 8
RET_DK = 128
RET_DV = 256
RET_QK = 1024
RET_V = 2048
ROPE_BASE = 10000.0
D_FF = 2816
FFN_CONV = 3

SUBLANES = 8
LANES = 128
GROUP_LANES = SSM_HEADS // SSM_GROUPS * SSM_HEAD_DIM
DT_LANES = 128
VMEM_LIMIT = 56 * 1024 * 1024


def _dot(a, b):
    return jnp.dot(a, b, preferred_element_type=F32)


def _dot_nt(a, b):
    return lax.dot_general(a, b, (((1,), (1,)), ((), ())), preferred_element_type=F32)


def _dot_tn(a, b):
    return lax.dot_general(a, b, (((0,), (0,)), ((), ())), preferred_element_type=F32)


def _rms(x, g):
    return x * lax.rsqrt(jnp.mean(x * x, axis=-1, keepdims=True) + EPS) * g


def _silu(x):
    return x * jax.nn.sigmoid(x)


def _softplus(x):
    return jnp.maximum(x, 0.0) + jnp.log1p(jnp.exp(-jnp.abs(x)))


def _pad_store(pad_ref, start, value):
    rows = value.shape[1]
    for j in range(pad_ref.shape[1]):
        pad_ref[:, j, start:start + rows, :] = value[:, :, j * LANES:(j + 1) * LANES]


def _pad_load(pad_ref, start, rows, j0=0, j1=None):
    j1 = pad_ref.shape[1] if j1 is None else j1
    return jnp.concatenate([pad_ref[:, j, start:start + rows, :] for j in range(j0, j1)], axis=-1)


def _causal_conv(pad_ref, w_ref, b_ref, tl, j0=0, j1=None):
    j1 = pad_ref.shape[1] if j1 is None else j1
    width = w_ref.shape[0]
    lanes = slice(j0 * LANES, j1 * LANES)
    y = b_ref[:, lanes][None]
    for k in range(width):
        y = y + w_ref[k:k + 1, lanes][None] * _pad_load(pad_ref, SUBLANES - (width - 1) + k, tl, j0, j1)
    return y


class _Cols:
    def __init__(self, src):
        self.src = src

    def cols(self, a, b):
        return self.src[:, a:b]


PROJ_BLOCK = 512


def _proj_spec(outputs):
    spec = []
    for o, (w_ref, first, n_cols, _) in enumerate(outputs):
        for c in range(0, n_cols, PROJ_BLOCK):
            spec.append((o, w_ref, first + c, c, min(PROJ_BLOCK, n_cols - c)))
    return spec


def _skewed(t, norm_fn, outputs, body_fn, init_fn, x_refs, bufs, nl):
    spec = _proj_spec(outputs)
    if bufs is None:
        (x_ref,) = x_refs

        @pl.when(t % nl == 0)
        def _():
            init_fn()

        x = x_ref[...].reshape(-1, D_MODEL)
        h = norm_fn(x)
        parts = [[] for _ in outputs]
        for o, w_ref, wc, _, n in spec:
            parts[o].append(_dot(h, w_ref[:, wc:wc + n]).astype(outputs[o][3]))
        body_fn(x, tuple(_Cols(jnp.concatenate(p, axis=1) if len(p) > 1 else p[0]) for p in parts), lambda n=1: None)
        return

    xa_ref, xb_ref = x_refs

    @pl.when(t == 0)
    def _():
        for r in bufs[1]:
            r[...] = jnp.zeros(r.shape, r.dtype)

    @pl.when(jnp.maximum(t - 1, 0) % nl == 0)
    def _():
        init_fn()

    for parity in range(2):
        @pl.when(t % 2 == parity)
        def _(parity=parity):
            h = norm_fn(xa_ref[...].reshape(-1, D_MODEL))
            pieces = iter(spec)

            def tick(n=1):
                for _ in range(n):
                    piece = next(pieces, None)
                    if piece is not None:
                        o, w_ref, wc, oc, width = piece
                        bufs[parity][o][:, oc:oc + width] = _dot(h, w_ref[:, wc:wc + width]).astype(outputs[o][3])

            body_fn(xb_ref[...].reshape(-1, D_MODEL), tuple(_Cols(r) for r in bufs[1 - parity]), tick)
            tick(len(spec))


def _split_refs(refs, n_x, n_in, n_out, n_scratch):
    refs = list(refs)
    x_refs, refs = refs[:n_x], refs[n_x:]
    ins, refs = refs[:n_in], refs[n_in:]
    outs, refs = refs[:n_out], refs[n_out:]
    scratch, refs = refs[:n_scratch], refs[n_scratch:]
    bufs = (refs[:len(refs) // 2], refs[len(refs) // 2:]) if refs else None
    return x_refs, ins, outs, scratch, bufs


def _mixer0_kernel(*refs, ns, tl, nl, tc, n_gm, emit_gmv, skew):
    x_refs, ins, outs, scratch, bufs = _split_refs(refs, 2 if skew else 1, 17, 4 if emit_gmv else 3, 5)
    (sconv_ref, sssm_ref, nm_ref, wuvz_ref, wxbc_ref, wdt_ref, dtb_ref, alog_ref, expd_ref, dskip_ref, cw_ref,
     cb_ref, snorm_ref, vgain_ref, gmw_ref, gmb_ref, wout_ref) = ins
    xo_ref, convo_ref, ht_ref = outs[:3]
    gmv_ref = outs[3] if emit_gmv else None
    xpad_ref, xs_ref, cs_ref, y_ref, mix_ref = scratch
    m = ns * tl
    nck = tl // tc
    nchunks = m // tc
    gm = GM_GROUPS * GM_GDIM

    def init():
        _pad_store(xpad_ref, 0, sconv_ref[...])
        ht_ref[...] = sssm_ref[...]

    def norm(x):
        return _rms(x, nm_ref[...]).astype(BF16)

    outputs = [(wuvz_ref, 0, 2 * gm + SSM_DIM, F32), (wxbc_ref, 0, SSM_CONV_DIM, F32), (wdt_ref, 0, DT_LANES, F32)]

    def body(x, projected, tick):
        uvz, xbc, dtr = projected

        ri = lax.broadcasted_iota(jnp.int32, (n_gm, n_gm), 0)
        ci = lax.broadcasted_iota(jnp.int32, (n_gm, n_gm), 1)
        gm_mask = (ci // CHUNK) <= (ri // CHUNK)
        for g in range(GM_GROUPS):
            lanes = slice(g * GM_GDIM, (g + 1) * GM_GDIM)
            u_g = jax.nn.gelu(uvz.cols(g * GM_GDIM, (g + 1) * GM_GDIM))
            v_g = jax.nn.gelu(uvz.cols(gm + g * GM_GDIM, gm + (g + 1) * GM_GDIM))
            v_g = v_g * lax.rsqrt(jnp.mean(v_g * v_g, axis=-1, keepdims=True) + EPS) * vgain_ref[:, lanes]
            if emit_gmv:
                gmv_ref[:, :, lanes] = v_g.reshape(ns, tl, GM_GDIM)
            w_g = jnp.where(gm_mask, gmw_ref[g], 0.0).astype(BF16)
            v_gb = v_g.astype(BF16)
            for c in range(m // n_gm):
                rows = slice(c * n_gm, (c + 1) * n_gm)
                s = _dot(w_g, v_gb[rows]) + gmb_ref[:, lanes]
                mix_ref[rows, lanes] = (u_g[rows] * s).astype(BF16)
            if g % 2 == 1:
                tick()
        x = x + _dot(mix_ref[:, :gm], wout_ref[:gm, :])

        for j in range(SSM_CONV_DIM // LANES):
            xpad_ref[:, j, SUBLANES:, :] = xbc.cols(j * LANES, (j + 1) * LANES).reshape(ns, tl, LANES)
        xc = _silu(_causal_conv(xpad_ref, cw_ref, cb_ref, tl)).reshape(m, SSM_CONV_DIM)
        tail = _pad_load(xpad_ref, tl, SUBLANES)
        _pad_store(xpad_ref, 0, tail)
        convo_ref[...] = tail
        tick()
        xh = xc[:, :SSM_DIM]
        bm = xc[:, SSM_DIM:SSM_DIM + SSM_GROUPS * SSM_STATE].astype(BF16)
        cm = xc[:, SSM_DIM + SSM_GROUPS * SSM_STATE:].astype(BF16)

        dt_n = _softplus(dtr.cols(0, DT_LANES) + dtb_ref[...])
        da_n = dt_n * (-jnp.exp(alog_ref[...]))

        def split3(v):
            hi = v.astype(BF16).astype(F32)
            r1 = v - hi
            mid = r1.astype(BF16).astype(F32)
            return hi, mid, r1 - mid

        lane_grp = lax.broadcasted_iota(jnp.int32, (m, DT_LANES), 1) // SSM_HEADS

        def expand(v):
            hi, mid, lo = split3(v)
            parts = jnp.where(lane_grp == 0, hi, jnp.where(lane_grp == 1, mid, lo))
            return _dot(parts.astype(BF16), expd_ref[...])

        ri = lax.broadcasted_iota(jnp.int32, (m, m), 0)
        ci = lax.broadcasted_iota(jnp.int32, (m, m), 1)
        tri = jnp.where(((ri // tc) == (ci // tc)) & (ci <= ri), 1.0, 0.0).astype(BF16)
        cs3 = _dot(tri, jnp.concatenate(split3(da_n), axis=1).astype(BF16))
        cs_n = cs3[:, :DT_LANES] + cs3[:, DT_LANES:2 * DT_LANES] + cs3[:, 2 * DT_LANES:]
        cs_ref[...] = expand(cs_n)
        dtx = expand(dt_n)
        xs_ref[...] = xh * dtx
        y_ref[...] = dskip_ref[...] * xh
        tick()

        def dup_rows(b_g):
            parts = []
            for c in range(nchunks):
                blk = b_g[c * tc:(c + 1) * tc]
                if tc < CHUNK:
                    blk = jnp.concatenate([blk, jnp.zeros((CHUNK - tc, SSM_STATE), BF16)], axis=0)
                parts += [blk, blk]
            return jnp.concatenate(parts, axis=0)

        cb2 = [_dot_nt(cm[:, g * SSM_STATE:(g + 1) * SSM_STATE], dup_rows(bm[:, g * SSM_STATE:(g + 1) * SSM_STATE]))
               for g in range(SSM_GROUPS)]

        t_i = lax.broadcasted_iota(jnp.int32, (tc, SSM_DIM), 0)
        s_i = lax.broadcasted_iota(jnp.int32, (tc, SSM_DIM), 1) % SSM_HEAD_DIM
        causal = s_i <= t_i
        diag = s_i == t_i
        quad_head = lax.broadcasted_iota(jnp.int32, (tc, 4 * SSM_HEAD_DIM), 1) // SSM_HEAD_DIM
        heads_per_group = SSM_HEADS // SSM_GROUPS

        for c in range(nchunks):
            seq = c // nck
            rows = slice(c * tc, (c + 1) * tc)
            cs_c = cs_ref[rows, :]
            cs_row = jnp.sum(jnp.where(diag, cs_c, 0.0), axis=0, keepdims=True)
            cs_last = cs_c[tc - 1:tc, :]
            lmat = jnp.exp(jnp.where(causal, cs_c - cs_row, -1e30))
            cb_c = jnp.concatenate(
                [cb2[g][rows, c * 2 * CHUNK:(c + 1) * 2 * CHUNK] for g in range(SSM_GROUPS)
                 for _ in range(heads_per_group // 2)], axis=1)
            mm = (lmat * cb_c).astype(BF16)
            xs_c = xs_ref[rows, :]
            y_parts = []
            for q in range(SSM_HEADS // 4):
                ql = slice(q * 4 * SSM_HEAD_DIM, (q + 1) * 4 * SSM_HEAD_DIM)
                xq = xs_c[:, ql]
                blocks = []
                for hh in range(4):
                    blk = jnp.where(quad_head == hh, xq, 0.0).astype(BF16)
                    if tc < CHUNK:
                        blk = jnp.concatenate([blk, jnp.zeros((CHUNK - tc, 4 * SSM_HEAD_DIM), BF16)], axis=0)
                    blocks.append(blk)
                y_parts.append(_dot(mm[:, ql], jnp.concatenate(blocks, axis=0)))
            y_c = jnp.concatenate(y_parts, axis=1)
            ecs = jnp.exp(cs_c)
            xsd = (xs_c * jnp.exp(cs_last - cs_c)).astype(BF16)
            e_last = ecs[tc - 1:tc, :]
            y_off = []
            for g in range(SSM_GROUPS):
                gl = slice(g * GROUP_LANES, (g + 1) * GROUP_LANES)
                sl = slice(g * SSM_STATE, (g + 1) * SSM_STATE)
                h_prev = ht_ref[seq, g]
                y_off.append(_dot(cm[rows, sl], h_prev.astype(BF16)))
                ht_ref[seq, g] = h_prev * e_last[:, gl] + _dot_tn(bm[rows, sl], xsd[:, gl])
            y_ref[rows, :] = y_ref[rows, :] + y_c + jnp.concatenate(y_off, axis=1) * ecs
            tick()

        yb = _rms(y_ref[...] * _silu(uvz.cols(2 * gm, 2 * gm + SSM_DIM)), snorm_ref[...])
        xo_ref[...] = (x + _dot(yb.astype(BF16), wout_ref[gm:, :])).reshape(ns, tl, D_MODEL)

    _skewed(pl.program_id(0), norm, outputs, body, init, x_refs, bufs, nl)


def _mixer1_kernel(*refs, ns, tl, nl, skew):
    x_refs, ins, outs, scratch, bufs = _split_refs(refs, 2 if skew else 1, 11, 2, 1)
    (sret_ref, nm_ref, win_ref, cos_ref, sin_ref, dmat_ref, qdec_ref, kdec_ref, cdec_ref, rnorm_ref,
     wout_ref) = ins
    xo_ref, s_ref = outs
    (mix_ref,) = scratch
    m = ns * tl

    def init():
        s_ref[...] = sret_ref[...]

    def norm(x):
        return _rms(x, nm_ref[...]).astype(BF16)

    outputs = [(win_ref, 0, 2 * RET_QK, F32), (win_ref, 2 * RET_QK, RET_V, BF16), (win_ref, 2 * RET_QK + RET_V, RET_V, F32)]

    def body(x, projected, tick):
        qk, vv, gg = projected
        cos = jnp.concatenate([cos_ref[...]] * ns, axis=0) if ns > 1 else cos_ref[...]
        sin = jnp.concatenate([sin_ref[...]] * ns, axis=0) if ns > 1 else sin_ref[...]
        qdec = qdec_ref[...]
        kdec = kdec_ref[...]
        scale = RET_DK ** -0.5
        for hd in range(RET_HEADS):
            kl = slice(hd * RET_DK, (hd + 1) * RET_DK)
            vl = slice(hd * RET_DV, (hd + 1) * RET_DV)
            q = qk.cols(hd * RET_DK, (hd + 1) * RET_DK)
            k = qk.cols(RET_QK + hd * RET_DK, RET_QK + (hd + 1) * RET_DK)
            v = vv.cols(hd * RET_DV, (hd + 1) * RET_DV)
            gate = gg.cols(hd * RET_DV, (hd + 1) * RET_DV)
            q = q * cos + pltpu.roll(q, RET_DK // 2, axis=1) * sin
            k = (k * cos + pltpu.roll(k, RET_DK // 2, axis=1) * sin) * scale
            sc = _dot_nt(q.astype(BF16), k.astype(BF16)) * dmat_ref[hd]
            y = _dot(sc.astype(BF16), v)
            qd = (q * qdec[:, kl]).astype(BF16)
            kd = (k * kdec[:, kl]).astype(BF16)
            y_inter = []
            for s in range(ns):
                rows = slice(s * tl, (s + 1) * tl)
                s_prev = s_ref[s, hd]
                y_inter.append(_dot(qd[rows], s_prev.astype(BF16)))
                s_ref[s, hd] = s_prev * cdec_ref[:, vl] + _dot_tn(kd[rows], v[rows])
            y = y + (jnp.concatenate(y_inter, axis=0) if ns > 1 else y_inter[0])
            if hd % 2 == 0:
                tick()
            mu = jnp.mean(y, axis=-1, keepdims=True)
            yc = y - mu
            var = jnp.mean(yc * yc, axis=-1, keepdims=True)
            yn = yc * lax.rsqrt(var + EPS) * rnorm_ref[:, vl]
            mix_ref[:, vl] = (_silu(gate) * yn).astype(BF16)
            tick()
            if hd % 2 == 1:
                pair = slice((hd - 1) * RET_DV, (hd + 1) * RET_DV)
                x = x + _dot(mix_ref[:, pair], wout_ref[pair, :])
        xo_ref[...] = x.reshape(ns, tl, D_MODEL)

    _skewed(pl.program_id(0), norm, outputs, body, init, x_refs, bufs, nl)


def _ffn_kernel(*refs, ns, tl, nl, final_norm, skew):
    x_refs, ins, outs, scratch, bufs = _split_refs(refs, 2 if skew else 1, 12, 2, 1)
    (p_ref, sbuf_ref, nf_ref, wg_ref, wu_ref, cw_ref, cb_ref, wd_ref, npl_ref, wpg_ref, wpp_ref, nfin_ref) = ins
    xo_ref, bufo_ref = outs
    (gpad_ref,) = scratch
    m = ns * tl

    def init():
        _pad_store(gpad_ref, 0, sbuf_ref[...])

    def norm(x):
        return _rms(x, nf_ref[...]).astype(BF16)

    outputs = []

    def body(x, projected, tick):
        pp = _dot(p_ref[...].reshape(m, PLE_DIM).astype(BF16), wpp_ref[...])
        h = norm(x)
        _pad_store(gpad_ref, SUBLANES, _dot(h, wg_ref[...]).reshape(ns, tl, D_FF))
        gate = _causal_conv(gpad_ref, cw_ref, cb_ref, tl).reshape(m, D_FF)
        tail = _pad_load(gpad_ref, tl, SUBLANES)
        _pad_store(gpad_ref, 0, tail)
        bufo_ref[...] = tail
        act = (_silu(gate) * _dot(h, wu_ref[...])).astype(BF16)
        x = x + _dot(act, wd_ref[...])
        pgate = jax.nn.sigmoid(_dot(_rms(x, npl_ref[...]).astype(BF16), wpg_ref[...]))
        x = x + pgate * pp
        if final_norm:
            x = _rms(x, nfin_ref[...])
        xo_ref[...] = x.reshape(ns, tl, D_MODEL)

    _skewed(pl.program_id(0), norm, outputs, body, init, x_refs, bufs, nl)


class _Tiling:
    def __init__(self, bsz, length, ns, tl, skew):
        self.ns, self.tl, self.skew = ns, tl, skew
        self.nl = length // tl
        self.n_tiles = (bsz // ns) * self.nl
        self.grid = (self.n_tiles + 1,) if skew else (self.n_tiles,)

    def body_tile(self, t):
        return jnp.maximum(t - 1, 0) if self.skew else t

    def proj_tile(self, t):
        return jnp.minimum(t, self.n_tiles - 1)

    def x_specs(self, width):
        blk = (self.ns, self.tl, width)
        body = pl.BlockSpec(blk, lambda t: (self.body_tile(t) // self.nl, self.body_tile(t) % self.nl, 0))
        if not self.skew:
            return [body]
        return [pl.BlockSpec(blk, lambda t: (self.proj_tile(t) // self.nl, self.proj_tile(t) % self.nl, 0)), body]

    def tile(self, width):
        return self.x_specs(width)[-1]

    def state(self, tail):
        zeros = (0,) * len(tail)
        return pl.BlockSpec((self.ns,) + tail, lambda t: (self.body_tile(t) // self.nl,) + zeros)

    def const(self, shape):
        zeros = (0,) * len(shape)
        return pl.BlockSpec(shape, lambda t: zeros, pipeline_mode=pl.Buffered(1))

    def bufs(self, shapes):
        return [pltpu.VMEM(s, d) for _ in range(2) for s, d in shapes] if self.skew else []


def _params():
    return pltpu.CompilerParams(dimension_semantics=("arbitrary",), vmem_limit_bytes=VMEM_LIMIT)


def _mixer0(x, sconv, sssm, w, *, ns, tl, emit_gmv, skew):
    bsz, length, _ = x.shape
    tg = _Tiling(bsz, length, ns, tl, skew)
    tc = min(tl, CHUNK)
    n_gm = min(tl, GM_CHUNK)
    m = ns * tl
    consts = [w['nm0'], w['wuvz'], w['wxbc'], w['wdt'], w['dtb'], w['alog'], w['expd'], w['dskip'], w['ssm_cw'],
              w['ssm_cb'], w['snorm'], w['vgain'], w['gmw'][:, :n_gm, :n_gm], w['gmb'][:n_gm], w['wout0']]
    out_shape = [jax.ShapeDtypeStruct(x.shape, F32),
                 jax.ShapeDtypeStruct((bsz, SUBLANES, SSM_CONV_DIM), F32),
                 jax.ShapeDtypeStruct(sssm.shape, F32)]
    out_specs = [tg.tile(D_MODEL), tg.state((SUBLANES, SSM_CONV_DIM)), tg.state(sssm.shape[1:])]
    if emit_gmv:
        out_shape.append(jax.ShapeDtypeStruct((bsz, length, GM_GROUPS * GM_GDIM), F32))
        out_specs.append(tg.tile(GM_GROUPS * GM_GDIM))
    return pl.pallas_call(
        functools.partial(_mixer0_kernel, ns=ns, tl=tl, nl=tg.nl, tc=tc, n_gm=n_gm, emit_gmv=emit_gmv, skew=skew),
        grid=tg.grid,
        in_specs=tg.x_specs(D_MODEL) + [tg.state((SUBLANES, SSM_CONV_DIM)), tg.state(sssm.shape[1:])]
        + [tg.const(c.shape) for c in consts],
        out_specs=out_specs,
        out_shape=out_shape,
        scratch_shapes=[pltpu.VMEM((ns, SSM_CONV_DIM // LANES, SUBLANES + tl, LANES), F32),
                        pltpu.VMEM((m, SSM_DIM), F32), pltpu.VMEM((m, SSM_DIM), F32),
                        pltpu.VMEM((m, SSM_DIM), F32), pltpu.VMEM((m, GM_GROUPS * GM_GDIM), BF16)]
        + tg.bufs([((m, 2 * GM_GROUPS * GM_GDIM + SSM_DIM), F32), ((m, SSM_CONV_DIM), F32), ((m, DT_LANES), F32)]),
        compiler_params=_params(),
        name="mixer0",
    )(*([x] * len(tg.x_specs(D_MODEL))), sconv, sssm, *consts)


def _mixer1(x, sret, w, tabs, *, ns, tl, skew):
    bsz, length, _ = x.shape
    tg = _Tiling(bsz, length, ns, tl, skew)
    m = ns * tl
    consts_a = [w['nm1'], w['win1']]
    consts_b = [tabs['dmat'], tabs['qdec'], tabs['kdec'], tabs['cdec'], w['rnorm'], w['wout1']]
    pos_spec = pl.BlockSpec((tl, RET_DK), lambda t: (tg.body_tile(t) % tg.nl, 0))
    return pl.pallas_call(
        functools.partial(_mixer1_kernel, ns=ns, tl=tl, nl=tg.nl, skew=skew),
        grid=tg.grid,
        in_specs=tg.x_specs(D_MODEL) + [tg.state(sret.shape[1:])]
        + [tg.const(c.shape) for c in consts_a] + [pos_spec, pos_spec]
        + [tg.const(c.shape) for c in consts_b],
        out_specs=[tg.tile(D_MODEL), tg.state(sret.shape[1:])],
        out_shape=[jax.ShapeDtypeStruct(x.shape, F32), jax.ShapeDtypeStruct(sret.shape, F32)],
        scratch_shapes=[pltpu.VMEM((m, RET_V), BF16)]
        + tg.bufs([((m, 2 * RET_QK), F32), ((m, RET_V), BF16), ((m, RET_V), F32)]),
        compiler_params=_params(),
        name="mixer1",
    )(*([x] * len(tg.x_specs(D_MODEL))), sret, *consts_a, tabs['cos'], tabs['sin'], *consts_b)


def _ffn(x, p, sbuf, w, layer, *, ns, tl, final_norm, skew):
    bsz, length, _ = x.shape
    tg = _Tiling(bsz, length, ns, tl, skew)
    m = ns * tl
    consts = [w['nf'][layer], w['wg'][layer], w['wu'][layer], w['ffn_cw'][layer], w['ffn_cb'][layer],
              w['wd'][layer], w['npl'][layer], w['wpg'][layer], w['wpp'][layer], w['nfin']]
    p_spec = pl.BlockSpec((None, ns, tl, PLE_DIM),
                          lambda t: (layer, tg.body_tile(t) // tg.nl, tg.body_tile(t) % tg.nl, 0))
    return pl.pallas_call(
        functools.partial(_ffn_kernel, ns=ns, tl=tl, nl=tg.nl, final_norm=final_norm, skew=skew),
        grid=tg.grid,
        in_specs=tg.x_specs(D_MODEL) + [p_spec, tg.state((SUBLANES, D_FF))] + [tg.const(c.shape) for c in consts],
        out_specs=[tg.tile(D_MODEL), tg.state((SUBLANES, D_FF))],
        out_shape=[jax.ShapeDtypeStruct(x.shape, F32), jax.ShapeDtypeStruct((bsz, SUBLANES, D_FF), F32)],
        scratch_shapes=[pltpu.VMEM((ns, D_FF // LANES, SUBLANES + tl, LANES), F32)]
        + tg.bufs([((m, D_FF), F32), ((m, D_FF), F32)]),
        compiler_params=_params(),
        name="ffn%d" % layer,
    )(*([x] * len(tg.x_specs(D_MODEL))), p, sbuf, *consts)


def _retention_tables(ns, tl, pos0, length):
    half = RET_DK // 2
    inv = ROPE_BASE ** (-jnp.arange(half, dtype=F32) / half)
    ang = (pos0 + jnp.arange(length)).astype(F32)[:, None] * inv[None]
    cos, sin = jnp.cos(ang), jnp.sin(ang)
    lg = jnp.log(1.0 - 2.0 ** (-5.0 - jnp.arange(RET_HEADS, dtype=F32)))
    m = ns * tl
    t = jnp.arange(m) % tl
    same_seq = (jnp.arange(m)[:, None] // tl) == (jnp.arange(m)[None, :] // tl)
    diff = (t[:, None] - t[None, :]).astype(F32)
    dmat = jnp.where((diff[None] >= 0) & same_seq[None],
                     jnp.exp(lg[:, None, None] * jnp.maximum(diff, 0.0)[None]), 0.0)
    tf = t.astype(F32)
    qdec = jnp.exp(lg[None, :] * (tf[:, None] + 1.0))
    kdec = jnp.exp(lg[None, :] * (tl - 1.0 - tf)[:, None])
    cdec = jnp.exp(lg * tl)
    return {
        'cos': jnp.concatenate([cos, cos], axis=-1),
        'sin': jnp.concatenate([-sin, sin], axis=-1),
        'dmat': dmat,
        'qdec': jnp.repeat(qdec, RET_DK, axis=1),
        'kdec': jnp.repeat(kdec, RET_DK, axis=1),
        'cdec': jnp.repeat(cdec, RET_DV)[None],
    }


def _header(buf):
    return jnp.pad(buf, ((0, 0), (SUBLANES - buf.shape[1], 0), (0, 0)))


def _trunk(x, p, sconv, sssm, sret, sffn, pos0, w, *, ns, ns_ret, tl, emit_gmv, skew):
    bsz, length, _ = x.shape
    hpg = SSM_HEADS // SSM_GROUPS
    sssm_t = sssm.reshape(bsz, SSM_GROUPS, hpg, SSM_HEAD_DIM, SSM_STATE).transpose(0, 1, 4, 2, 3)
    sssm_t = sssm_t.reshape(bsz, SSM_GROUPS, SSM_STATE, GROUP_LANES)
    outs = _mixer0(x, _header(sconv), sssm_t, w, ns=ns, tl=tl, emit_gmv=emit_gmv, skew=skew)
    x, conv_o, ssm_o = outs[:3]
    gmv = outs[3] if emit_gmv else None
    x, ffn0 = _ffn(x, p, _header(sffn[0]), w, 0, ns=ns, tl=tl, final_norm=False, skew=False)
    tabs = _retention_tables(ns_ret, tl, pos0, length)
    x, ret_o = _mixer1(x, sret, w, tabs, ns=ns_ret, tl=tl, skew=skew)
    x, ffn1 = _ffn(x, p, _header(sffn[1]), w, 1, ns=ns, tl=tl, final_norm=True, skew=False)
    ssm_o = ssm_o.reshape(bsz, SSM_GROUPS, SSM_STATE, hpg, SSM_HEAD_DIM).transpose(0, 1, 3, 4, 2)
    ssm_o = ssm_o.reshape(bsz, SSM_HEADS, SSM_HEAD_DIM, SSM_STATE)
    new_ffn = jnp.stack([ffn0[:, SUBLANES - (FFN_CONV - 1):], ffn1[:, SUBLANES - (FFN_CONV - 1):]])
    return x, gmv, conv_o[:, SUBLANES - (SSM_CONV - 1):], ssm_o, ret_o, new_ffn


def _prepare_weights(norm_mix, norm_ffn, norm_ple, norm_final, l0_w_in, gm_v_gain, gm_w_s, gm_b_s,
                     ssm_conv_w, ssm_conv_b, ssm_dt_bias, ssm_a_log, ssm_d, ssm_norm, l0_w_out,
                     l1_w_in, ret_norm, l1_w_out, ffn_w_gate, ffn_w_up, ffn_conv_w, ffn_conv_b,
                     ffn_w_down, ple_w_gate, ple_w_proj):
    gm = GM_GROUPS * GM_GDIM
    rep = lambda v: jnp.repeat(v, SSM_HEAD_DIM)[None].astype(F32)
    tile = lambda v: jnp.tile(v, DT_LANES // SSM_HEADS)[None].astype(F32)
    lane = jnp.arange(DT_LANES)
    return {
        'nm0': norm_mix[0][None], 'nm1': norm_mix[1][None],
        'wuvz': l0_w_in[:, :2 * gm + SSM_DIM].astype(BF16),
        'wxbc': l0_w_in[:, 2 * gm + SSM_DIM:2 * gm + SSM_DIM + SSM_CONV_DIM].astype(BF16),
        'wdt': jnp.tile(l0_w_in[:, 2 * gm + SSM_DIM + SSM_CONV_DIM:], (1, DT_LANES // SSM_HEADS)).astype(BF16),
        'dtb': tile(ssm_dt_bias), 'alog': tile(ssm_a_log), 'dskip': rep(ssm_d),
        'expd': ((lane[:, None] < 3 * SSM_HEADS)
                 & (lane[:, None] % SSM_HEADS == jnp.arange(SSM_DIM)[None] // SSM_HEAD_DIM)).astype(BF16),
        'ssm_cw': ssm_conv_w, 'ssm_cb': ssm_conv_b[None], 'snorm': ssm_norm[None], 'vgain': gm_v_gain[None],
        'gmw': gm_w_s, 'gmb': jnp.repeat(gm_b_s.T, GM_GDIM, axis=1),
        'wout0': l0_w_out.astype(BF16),
        'win1': l1_w_in.astype(BF16), 'rnorm': ret_norm[None], 'wout1': l1_w_out.astype(BF16),
        'nf': norm_ffn[:, None], 'wg': ffn_w_gate.astype(BF16), 'wu': ffn_w_up.astype(BF16),
        'ffn_cw': ffn_conv_w, 'ffn_cb': ffn_conv_b[:, None], 'wd': ffn_w_down.astype(BF16),
        'npl': norm_ple[:, None], 'wpg': ple_w_gate.astype(BF16), 'wpp': ple_w_proj.astype(BF16),
        'nfin': norm_final[None],
    }


def kernel(x_prompt, x_sample, state_ssm_conv, state_ssm, state_ret, state_ffn_conv, p_prompt, p_sample, norm_mix, norm_ffn, norm_ple, norm_final, l0_w_in, gm_v_gain, gm_w_s, gm_b_s, ssm_conv_w, ssm_conv_b, ssm_dt_bias, ssm_a_log, ssm_d, ssm_norm, l0_w_out, l1_w_in, ret_norm, l1_w_out, ffn_w_gate, ffn_w_up, ffn_conv_w, ffn_conv_b, ffn_w_down, ple_w_gate, ple_w_proj):
    w = _prepare_weights(norm_mix, norm_ffn, norm_ple, norm_final, l0_w_in, gm_v_gain, gm_w_s, gm_b_s,
                         ssm_conv_w, ssm_conv_b, ssm_dt_bias, ssm_a_log, ssm_d, ssm_norm, l0_w_out,
                         l1_w_in, ret_norm, l1_w_out, ffn_w_gate, ffn_w_up, ffn_conv_w, ffn_conv_b,
                         ffn_w_down, ple_w_gate, ple_w_proj)
    bp = x_prompt.shape[0]
    zero_conv = jnp.zeros((bp, SSM_CONV - 1, SSM_CONV_DIM), F32)
    zero_ssm = jnp.zeros((bp, SSM_HEADS, SSM_HEAD_DIM, SSM_STATE), F32)
    zero_ret = jnp.zeros((bp, RET_HEADS, RET_DK, RET_DV), F32)
    zero_ffn = jnp.zeros((DEPTH, bp, FFN_CONV - 1, D_FF), F32)
    y_p, _, conv_p, ssm_p, ret_p, ffn_p = _trunk(
        x_prompt, p_prompt, zero_conv, zero_ssm, zero_ret, zero_ffn, 0, w, ns=1, ns_ret=1, tl=256, emit_gmv=False,
        skew=True)
    y_s, gmv_s, conv_s, ssm_s, ret_s, ffn_s = _trunk(
        x_sample, p_sample, state_ssm_conv, state_ssm, state_ret, state_ffn_conv, PAST_LEN, w,
        ns=8, ns_ret=4, tl=32, emit_gmv=True, skew=False)
    return (y_p, y_s, conv_p, conv_s, ssm_p, ssm_s, ret_p, ret_s, ffn_p, ffn_s, gmv_s)
```

```python
import functools

import jax
import jax.numpy as jnp
import numpy as np
from jax import lax
from jax.experimental import pallas as pl
from jax.experimental.pallas import tpu as pltpu

F32 = jnp.float32
BF16 = jnp.bfloat16

D_MODEL = 1024
DEPTH = 2
PAST_LEN = 1024
CHUNK = 64
PLE_DIM = 256
EPS = 1e-6
GM_CHUNK = 128
GM_GROUPS = 8
GM_GDIM = 128
SSM_HEADS = 16
SSM_HEAD_DIM = 64
SSM_DIM = 1024
SSM_GROUPS = 2
SSM_STATE = 128
SSM_CONV = 4
SSM_CONV_DIM = 1536
RET_HEADS = 8
RET_DK = 128
RET_DV = 256
RET_QK = 1024
RET_V = 2048
ROPE_BASE = 10000.0
D_FF = 2816
FFN_CONV = 3

SUBLANES = 8
LANES = 128
GROUP_LANES = SSM_HEADS // SSM_GROUPS * SSM_HEAD_DIM
DT_LANES = 128
VMEM_LIMIT = 56 * 1024 * 1024


def _dot(a, b):
    return jnp.dot(a, b, preferred_element_type=F32)


def _dot_nt(a, b):
    return lax.dot_general(a, b, (((1,), (1,)), ((), ())), preferred_element_type=F32)


def _dot_tn(a, b):
    return lax.dot_general(a, b, (((0,), (0,)), ((), ())), preferred_element_type=F32)


def _rms(x, g):
    return x * lax.rsqrt(jnp.mean(x * x, axis=-1, keepdims=True) + EPS) * g


def _silu(x):
    return x * jax.nn.sigmoid(x)


def _softplus(x):
    return jnp.maximum(x, 0.0) + jnp.log1p(jnp.exp(-jnp.abs(x)))


def _pad_store(pad_ref, start, value):
    rows = value.shape[1]
    for j in range(pad_ref.shape[1]):
        pad_ref[:, j, start:start + rows, :] = value[:, :, j * LANES:(j + 1) * LANES]


def _pad_load(pad_ref, start, rows, j0=0, j1=None):
    j1 = pad_ref.shape[1] if j1 is None else j1
    return jnp.concatenate([pad_ref[:, j, start:start + rows, :] for j in range(j0, j1)], axis=-1)


def _causal_conv(pad_ref, w_ref, b_ref, tl, j0=0, j1=None):
    j1 = pad_ref.shape[1] if j1 is None else j1
    width = w_ref.shape[0]
    lanes = slice(j0 * LANES, j1 * LANES)
    y = b_ref[:, lanes][None]
    for k in range(width):
        y = y + w_ref[k:k + 1, lanes][None] * _pad_load(pad_ref, SUBLANES - (width - 1) + k, tl, j0, j1)
    return y


class _Cols:
    def __init__(self, src):
        self.src = src

    def cols(self, a, b):
        return self.src[:, a:b]


PROJ_BLOCK = 512


def _proj_spec(outputs):
    spec = []
    for o, (w_ref, first, n_cols, _) in enumerate(outputs):
        for c in range(0, n_cols, PROJ_BLOCK):
            spec.append((o, w_ref, first + c, c, min(PROJ_BLOCK, n_cols - c)))
    return spec


def _skewed(t, norm_fn, outputs, body_fn, init_fn, x_refs, bufs, nl, final_fn=None):
    spec = _proj_spec(outputs)
    if bufs is None:
        (x_ref,) = x_refs

        @pl.when(t % nl == 0)
        def _():
            init_fn()

        x = x_ref[...].reshape(-1, D_MODEL)
        h = norm_fn(x)
        parts = [[] for _ in outputs]
        for o, w_ref, wc, _, n in spec:
            parts[o].append(_dot(h, w_ref[:, wc:wc + n]).astype(outputs[o][3]))
        body_fn(x, tuple(_Cols(jnp.concatenate(p, axis=1) if len(p) > 1 else p[0]) for p in parts), lambda n=1: None)
        if final_fn is not None:
            @pl.when(t % nl == nl - 1)
            def _():
                final_fn()
        return

    xa_ref, xb_ref = x_refs

    @pl.when(t == 0)
    def _():
        for r in bufs[1]:
            r[...] = jnp.zeros(r.shape, r.dtype)

    @pl.when(jnp.maximum(t - 1, 0) % nl == 0)
    def _():
        init_fn()

    for parity in range(2):
        @pl.when(t % 2 == parity)
        def _(parity=parity):
            h = norm_fn(xa_ref[...].reshape(-1, D_MODEL))
            pieces = iter(spec)

            def tick(n=1):
                for _ in range(n):
                    piece = next(pieces, None)
                    if piece is not None:
                        o, w_ref, wc, oc, width = piece
                        bufs[parity][o][:, oc:oc + width] = _dot(h, w_ref[:, wc:wc + width]).astype(outputs[o][3])

            body_fn(xb_ref[...].reshape(-1, D_MODEL), tuple(_Cols(r) for r in bufs[1 - parity]), tick)
            tick(len(spec))

    if final_fn is not None:
        @pl.when(jnp.maximum(t - 1, 0) % nl == nl - 1)
        def _():
            final_fn()


def _split_refs(refs, n_x, n_in, n_out, n_scratch):
    refs = list(refs)
    x_refs, refs = refs[:n_x], refs[n_x:]
    ins, refs = refs[:n_in], refs[n_in:]
    outs, refs = refs[:n_out], refs[n_out:]
    scratch, refs = refs[:n_scratch], refs[n_scratch:]
    bufs = (refs[:len(refs) // 2], refs[len(refs) // 2:]) if refs else None
    return x_refs, ins, outs, scratch, bufs


def _mixer0_kernel(*refs, ns, tl, nl, tc, n_gm, emit_gmv, skew, has_state):
    n_state = 2 if has_state else 0
    x_refs, ins, outs, scratch, bufs = _split_refs(refs, 2 if skew else 1, n_state + 15, 4 if emit_gmv else 3, 6)
    sconv_ref, sssm_ref = ins[:n_state] if has_state else (None, None)
    (nm_ref, wuvz_ref, wxbc_ref, wdt_ref, dtb_ref, alog_ref, expd_ref, dskip_ref, cw_ref,
     cb_ref, snorm_ref, vgain_ref, gmw_ref, gmb_ref, wout_ref) = ins[n_state:]
    xo_ref, convo_ref, sso_ref = outs[:3]
    gmv_ref = outs[3] if emit_gmv else None
    xpad_ref, xs_ref, cs_ref, y_ref, mix_ref, ht_ref = scratch
    pairs_per_group = SSM_HEADS // SSM_GROUPS // 2
    m = ns * tl
    nck = tl // tc
    nchunks = m // tc
    gm = GM_GROUPS * GM_GDIM

    def init():
        if not has_state:
            for j in range(xpad_ref.shape[1]):
                xpad_ref[:, j, 0:SUBLANES, :] = jnp.zeros((ns, SUBLANES, LANES), F32)
            ht_ref[...] = jnp.zeros(ht_ref.shape, F32)
            return
        _pad_store(xpad_ref, 0, sconv_ref[...])
        for s in range(ns):
            for pr in range(SSM_HEADS // 2):
                g, q = divmod(pr, pairs_per_group)
                ht_ref[s, g, :, q * 2 * SSM_HEAD_DIM:(q + 1) * 2 * SSM_HEAD_DIM] = sssm_ref[s, pr].T

    def final():
        for s in range(ns):
            for pr in range(SSM_HEADS // 2):
                g, q = divmod(pr, pairs_per_group)
                sso_ref[s, pr] = ht_ref[s, g, :, q * 2 * SSM_HEAD_DIM:(q + 1) * 2 * SSM_HEAD_DIM].T

    def norm(x):
        return _rms(x, nm_ref[...]).astype(BF16)

    outputs = [(wuvz_ref, 0, 2 * gm + SSM_DIM, F32), (wxbc_ref, 0, SSM_CONV_DIM, F32), (wdt_ref, 0, DT_LANES, F32)]

    def body(x, projected, tick):
        uvz, xbc, dtr = projected

        ri = lax.broadcasted_iota(jnp.int32, (n_gm, n_gm), 0)
        ci = lax.broadcasted_iota(jnp.int32, (n_gm, n_gm), 1)
        gm_mask = (ci // CHUNK) <= (ri // CHUNK)
        for g in range(GM_GROUPS):
            lanes = slice(g * GM_GDIM, (g + 1) * GM_GDIM)
            u_g = jax.nn.gelu(uvz.cols(g * GM_GDIM, (g + 1) * GM_GDIM))
            v_g = jax.nn.gelu(uvz.cols(gm + g * GM_GDIM, gm + (g + 1) * GM_GDIM))
            v_g = v_g * lax.rsqrt(jnp.mean(v_g * v_g, axis=-1, keepdims=True) + EPS) * vgain_ref[:, lanes]
            if emit_gmv:
                gmv_ref[:, :, lanes] = v_g.reshape(ns, tl, GM_GDIM)
            w_g = jnp.where(gm_mask, gmw_ref[g], 0.0).astype(BF16)
            v_gb = v_g.astype(BF16)
            for c in range(m // n_gm):
                rows = slice(c * n_gm, (c + 1) * n_gm)
                s = _dot(w_g, v_gb[rows]) + gmb_ref[:, lanes]
                mix_ref[rows, lanes] = (u_g[rows] * s).astype(BF16)
            if g % 2 == 1:
                tick()
        x = x + _dot(mix_ref[:, :gm], wout_ref[:gm, :])

        for j in range(SSM_CONV_DIM // LANES):
            xpad_ref[:, j, SUBLANES:, :] = xbc.cols(j * LANES, (j + 1) * LANES).reshape(ns, tl, LANES)
        xc = _silu(_causal_conv(xpad_ref, cw_ref, cb_ref, tl)).reshape(m, SSM_CONV_DIM)
        tail = _pad_load(xpad_ref, tl, SUBLANES)
        _pad_store(xpad_ref, 0, tail)
        convo_ref[...] = tail
        tick()
        xh = xc[:, :SSM_DIM]
        bm = xc[:, SSM_DIM:SSM_DIM + SSM_GROUPS * SSM_STATE].astype(BF16)
        cm = xc[:, SSM_DIM + SSM_GROUPS * SSM_STATE:].astype(BF16)

        dt_n = _softplus(dtr.cols(0, DT_LANES) + dtb_ref[...])
        da_n = dt_n * (-jnp.exp(alog_ref[...]))

        def split3(v):
            hi = v.astype(BF16).astype(F32)
            r1 = v - hi
            mid = r1.astype(BF16).astype(F32)
            return hi, mid, r1 - mid

        lane_grp = lax.broadcasted_iota(jnp.int32, (m, DT_LANES), 1) // SSM_HEADS

        def expand(v):
            hi, mid, lo = split3(v)
            parts = jnp.where(lane_grp == 0, hi, jnp.where(lane_grp == 1, mid, lo))
            return _dot(parts.astype(BF16), expd_ref[...])

        ri = lax.broadcasted_iota(jnp.int32, (m, m), 0)
        ci = lax.broadcasted_iota(jnp.int32, (m, m), 1)
        tri = jnp.where(((ri // tc) == (ci // tc)) & (ci <= ri), 1.0, 0.0).astype(BF16)
        cs3 = _dot(tri, jnp.concatenate(split3(da_n), axis=1).astype(BF16))
        cs_n = cs3[:, :DT_LANES] + cs3[:, DT_LANES:2 * DT_LANES] + cs3[:, 2 * DT_LANES:]
        cs_ref[...] = expand(cs_n)
        dtx = expand(dt_n)
        xs_ref[...] = xh * dtx
        y_ref[...] = dskip_ref[...] * xh
        tick()

        def dup_rows(b_g):
            parts = []
            for c in range(nchunks):
                blk = b_g[c * tc:(c + 1) * tc]
                if tc < CHUNK:
                    blk = jnp.concatenate([blk, jnp.zeros((CHUNK - tc, SSM_STATE), BF16)], axis=0)
                parts += [blk, blk]
            return jnp.concatenate(parts, axis=0)

        cb2 = [_dot_nt(cm[:, g * SSM_STATE:(g + 1) * SSM_STATE], dup_rows(bm[:, g * SSM_STATE:(g + 1) * SSM_STATE]))
               for g in range(SSM_GROUPS)]

        t_i = lax.broadcasted_iota(jnp.int32, (tc, SSM_DIM), 0)
        s_i = lax.broadcasted_iota(jnp.int32, (tc, SSM_DIM), 1) % SSM_HEAD_DIM
        causal = s_i <= t_i
        diag = s_i == t_i
        quad_head = lax.broadcasted_iota(jnp.int32, (tc, 4 * SSM_HEAD_DIM), 1) // SSM_HEAD_DIM
        heads_per_group = SSM_HEADS // SSM_GROUPS

        for c in range(nchunks):
            seq = c // nck
            rows = slice(c * tc, (c + 1) * tc)
            cs_c = cs_ref[rows, :]
            cs_row = jnp.sum(jnp.where(diag, cs_c, 0.0), axis=0, keepdims=True)
            cs_last = cs_c[tc - 1:tc, :]
            lmat = jnp.exp(jnp.where(causal, cs_c - cs_row, -1e30))
            cb_c = jnp.concatenate(
                [cb2[g][rows, c * 2 * CHUNK:(c + 1) * 2 * CHUNK] for g in range(SSM_GROUPS)
                 for _ in range(heads_per_group // 2)], axis=1)
            mm = (lmat * cb_c).astype(BF16)
            xs_c = xs_ref[rows, :]
            y_parts = []
            for q in range(SSM_HEADS // 4):
                ql = slice(q * 4 * SSM_HEAD_DIM, (q + 1) * 4 * SSM_HEAD_DIM)
                xq = xs_c[:, ql]
                blocks = []
                for hh in range(4):
                    blk = jnp.where(quad_head == hh, xq, 0.0).astype(BF16)
                    if tc < CHUNK:
                        blk = jnp.concatenate([blk, jnp.zeros((CHUNK - tc, 4 * SSM_HEAD_DIM), BF16)], axis=0)
                    blocks.append(blk)
                y_parts.append(_dot(mm[:, ql], jnp.concatenate(blocks, axis=0)))
            y_c = jnp.concatenate(y_parts, axis=1)
            ecs = jnp.exp(cs_c)
            xsd = (xs_c * jnp.exp(cs_last - cs_c)).astype(BF16)
            e_last = ecs[tc - 1:tc, :]
            y_off = []
            for g in range(SSM_GROUPS):
                gl = slice(g * GROUP_LANES, (g + 1) * GROUP_LANES)
                sl = slice(g * SSM_STATE, (g + 1) * SSM_STATE)
                h_prev = ht_ref[seq, g]
                y_off.append(_dot(cm[rows, sl], h_prev.astype(BF16)))
                ht_ref[seq, g] = h_prev * e_last[:, gl] + _dot_tn(bm[rows, sl], xsd[:, gl])
            y_ref[rows, :] = y_ref[rows, :] + y_c + jnp.concatenate(y_off, axis=1) * ecs
            tick()

        yb = _rms(y_ref[...] * _silu(uvz.cols(2 * gm, 2 * gm + SSM_DIM)), snorm_ref[...])
        xo_ref[...] = (x + _dot(yb.astype(BF16), wout_ref[gm:, :])).reshape(ns, tl, D_MODEL)

    _skewed(pl.program_id(0), norm, outputs, body, init, x_refs, bufs, nl, final)


def _mixer1_kernel(*refs, ns, tl, nl, skew, has_state):
    n_state = 1 if has_state else 0
    x_refs, ins, outs, scratch, bufs = _split_refs(refs, 2 if skew else 1, n_state + 10, 2, 1)
    sret_ref = ins[0] if has_state else None
    (nm_ref, win_ref, cos_ref, sin_ref, dmat_ref, qdec_ref, kdec_ref, cdec_ref, rnorm_ref,
     wout_ref) = ins[n_state:]
    xo_ref, s_ref = outs
    (mix_ref,) = scratch
    m = ns * tl

    def init():
        s_ref[...] = sret_ref[...] if has_state else jnp.zeros(s_ref.shape, F32)

    def norm(x):
        return _rms(x, nm_ref[...]).astype(BF16)

    outputs = [(win_ref, 0, 2 * RET_QK, F32), (win_ref, 2 * RET_QK, RET_V, BF16), (win_ref, 2 * RET_QK + RET_V, RET_V, F32)]

    def body(x, projected, tick):
        qk, vv, gg = projected
        cos = jnp.concatenate([cos_ref[...]] * ns, axis=0) if ns > 1 else cos_ref[...]
        sin = jnp.concatenate([sin_ref[...]] * ns, axis=0) if ns > 1 else sin_ref[...]
        qdec = qdec_ref[...]
        kdec = kdec_ref[...]
        scale = RET_DK ** -0.5
        for hd in range(RET_HEADS):
            kl = slice(hd * RET_DK, (hd + 1) * RET_DK)
            vl = slice(hd * RET_DV, (hd + 1) * RET_DV)
            q = qk.cols(hd * RET_DK, (hd + 1) * RET_DK)
            k = qk.cols(RET_QK + hd * RET_DK, RET_QK + (hd + 1) * RET_DK)
            v = vv.cols(hd * RET_DV, (hd + 1) * RET_DV)
            gate = gg.cols(hd * RET_DV, (hd + 1) * RET_DV)
            q = q * cos + pltpu.roll(q, RET_DK // 2, axis=1) * sin
            k = (k * cos + pltpu.roll(k, RET_DK // 2, axis=1) * sin) * scale
            sc = _dot_nt(q.astype(BF16), k.astype(BF16)) * dmat_ref[hd]
            y = _dot(sc.astype(BF16), v)
            qd = (q * qdec[:, kl]).astype(BF16)
            kd = (k * kdec[:, kl]).astype(BF16)
            y_inter = []
            for s in range(ns):
                rows = slice(s * tl, (s + 1) * tl)
                s_prev = s_ref[s, hd]
                y_inter.append(_dot(qd[rows], s_prev.astype(BF16)))
                s_ref[s, hd] = s_prev * cdec_ref[:, vl] + _dot_tn(kd[rows], v[rows])
            y = y + (jnp.concatenate(y_inter, axis=0) if ns > 1 else y_inter[0])
            if hd % 2 == 0:
                tick()
            mu = jnp.mean(y, axis=-1, keepdims=True)
            yc = y - mu
            var = jnp.mean(yc * yc, axis=-1, keepdims=True)
            yn = yc * lax.rsqrt(var + EPS) * rnorm_ref[:, vl]
            mix_ref[:, vl] = (_silu(gate) * yn).astype(BF16)
            tick()
            if hd % 2 == 1:
                pair = slice((hd - 1) * RET_DV, (hd + 1) * RET_DV)
                x = x + _dot(mix_ref[:, pair], wout_ref[pair, :])
        xo_ref[...] = x.reshape(ns, tl, D_MODEL)

    _skewed(pl.program_id(0), norm, outputs, body, init, x_refs, bufs, nl)


def _ffn_kernel(*refs, ns, tl, nl, final_norm, skew, has_state):
    n_state = 1 if has_state else 0
    x_refs, ins, outs, scratch, bufs = _split_refs(refs, 2 if skew else 1, n_state + 11, 2, 1)
    p_ref = ins[0]
    sbuf_ref = ins[1] if has_state else None
    (nf_ref, wg_ref, wu_ref, cw_ref, cb_ref, wd_ref, npl_ref, wpg_ref, wpp_ref, nfin_ref) = ins[1 + n_state:]
    xo_ref, bufo_ref = outs
    (gpad_ref,) = scratch
    m = ns * tl

    def init():
        if has_state:
            _pad_store(gpad_ref, 0, sbuf_ref[...])
        else:
            for j in range(gpad_ref.shape[1]):
                gpad_ref[:, j, 0:SUBLANES, :] = jnp.zeros((ns, SUBLANES, LANES), F32)

    def norm(x):
        return _rms(x, nf_ref[...]).astype(BF16)

    outputs = []

    def body(x, projected, tick):
        pp = _dot(p_ref[...].reshape(m, PLE_DIM).astype(BF16), wpp_ref[...])
        h = norm(x)
        _pad_store(gpad_ref, SUBLANES, _dot(h, wg_ref[...]).reshape(ns, tl, D_FF))
        gate = _causal_conv(gpad_ref, cw_ref, cb_ref, tl).reshape(m, D_FF)
        tail = _pad_load(gpad_ref, tl, SUBLANES)
        _pad_store(gpad_ref, 0, tail)
        bufo_ref[...] = tail
        act = (_silu(gate) * _dot(h, wu_ref[...])).astype(BF16)
        x = x + _dot(act, wd_ref[...])
        pgate = jax.nn.sigmoid(_dot(_rms(x, npl_ref[...]).astype(BF16), wpg_ref[...]))
        x = x + pgate * pp
        if final_norm:
            x = _rms(x, nfin_ref[...])
        xo_ref[...] = x.reshape(ns, tl, D_MODEL)

    _skewed(pl.program_id(0), norm, outputs, body, init, x_refs, bufs, nl)


class _Tiling:
    def __init__(self, bsz, length, ns, tl, skew):
        self.ns, self.tl, self.skew = ns, tl, skew
        self.nl = length // tl
        self.n_tiles = (bsz // ns) * self.nl
        self.grid = (self.n_tiles + 1,) if skew else (self.n_tiles,)

    def body_tile(self, t):
        return jnp.maximum(t - 1, 0) if self.skew else t

    def proj_tile(self, t):
        return jnp.minimum(t, self.n_tiles - 1)

    def x_specs(self, width):
        blk = (self.ns, self.tl, width)
        body = pl.BlockSpec(blk, lambda t: (self.body_tile(t) // self.nl, self.body_tile(t) % self.nl, 0))
        if not self.skew:
            return [body]
        return [pl.BlockSpec(blk, lambda t: (self.proj_tile(t) // self.nl, self.proj_tile(t) % self.nl, 0)), body]

    def tile(self, width):
        return self.x_specs(width)[-1]

    def state(self, tail):
        zeros = (0,) * len(tail)
        return pl.BlockSpec((self.ns,) + tail, lambda t: (self.body_tile(t) // self.nl,) + zeros)

    def const(self, shape):
        zeros = (0,) * len(shape)
        return pl.BlockSpec(shape, lambda t: zeros, pipeline_mode=pl.Buffered(1))

    def bufs(self, shapes):
        return [pltpu.VMEM(s, d) for _ in range(2) for s, d in shapes] if self.skew else []


def _params():
    return pltpu.CompilerParams(dimension_semantics=("arbitrary",), vmem_limit_bytes=VMEM_LIMIT)


def _mixer0(x, sconv, sssm, w, *, ns, tl, emit_gmv, skew):
    bsz, length, _ = x.shape
    tg = _Tiling(bsz, length, ns, tl, skew)
    tc = min(tl, CHUNK)
    n_gm = min(tl, GM_CHUNK)
    m = ns * tl
    has_state = sconv is not None
    pair_shape = (SSM_HEADS // 2, 2 * SSM_HEAD_DIM, SSM_STATE)
    states = [_header(sconv), sssm.reshape((bsz,) + pair_shape)] if has_state else []
    state_specs = [tg.state((SUBLANES, SSM_CONV_DIM)), tg.state(pair_shape)]
    consts = [w['nm0'], w['wuvz'], w['wxbc'], w['wdt'], w['dtb'], w['alog'], w['expd'], w['dskip'], w['ssm_cw'],
              w['ssm_cb'], w['snorm'], w['vgain'], w['gmw'][:, :n_gm, :n_gm], w['gmb'][:n_gm], w['wout0']]
    out_shape = [jax.ShapeDtypeStruct(x.shape, F32),
                 jax.ShapeDtypeStruct((bsz, SUBLANES, SSM_CONV_DIM), F32),
                 jax.ShapeDtypeStruct((bsz,) + pair_shape, F32)]
    out_specs = [tg.tile(D_MODEL)] + state_specs
    if emit_gmv:
        out_shape.append(jax.ShapeDtypeStruct((bsz, length, GM_GROUPS * GM_GDIM), F32))
        out_specs.append(tg.tile(GM_GROUPS * GM_GDIM))
    outs = pl.pallas_call(
        functools.partial(_mixer0_kernel, ns=ns, tl=tl, nl=tg.nl, tc=tc, n_gm=n_gm, emit_gmv=emit_gmv, skew=skew,
                          has_state=has_state),
        grid=tg.grid,
        in_specs=tg.x_specs(D_MODEL) + state_specs[:len(states)] + [tg.const(c.shape) for c in consts],
        out_specs=out_specs,
        out_shape=out_shape,
        scratch_shapes=[pltpu.VMEM((ns, SSM_CONV_DIM // LANES, SUBLANES + tl, LANES), F32),
                        pltpu.VMEM((m, SSM_DIM), F32), pltpu.VMEM((m, SSM_DIM), F32),
                        pltpu.VMEM((m, SSM_DIM), F32), pltpu.VMEM((m, GM_GROUPS * GM_GDIM), BF16),
                        pltpu.VMEM((ns, SSM_GROUPS, SSM_STATE, GROUP_LANES), F32)]
        + tg.bufs([((m, 2 * GM_GROUPS * GM_GDIM + SSM_DIM), F32), ((m, SSM_CONV_DIM), F32), ((m, DT_LANES), F32)]),
        compiler_params=_params(),
        name="mixer0",
    )(*([x] * len(tg.x_specs(D_MODEL))), *states, *consts)
    conv_o = outs[1][:, SUBLANES - (SSM_CONV - 1):]
    ssm_o = outs[2].reshape(bsz, SSM_HEADS, SSM_HEAD_DIM, SSM_STATE)
    return outs[0], conv_o, ssm_o, (outs[3] if emit_gmv else None)


def _mixer1(x, sret, w, pos0, *, ns, tl, skew):
    bsz, length, _ = x.shape
    tg = _Tiling(bsz, length, ns, tl, skew)
    m = ns * tl
    tabs = _retention_tables(ns, tl, pos0, length)
    has_state = sret is not None
    state_shape = (RET_HEADS, RET_DK, RET_DV)
    states = [sret] if has_state else []
    consts_a = [w['nm1'], w['win1']]
    consts_b = [tabs['dmat'], tabs['qdec'], tabs['kdec'], tabs['cdec'], w['rnorm'], w['wout1']]
    pos_spec = pl.BlockSpec((tl, RET_DK), lambda t: (tg.body_tile(t) % tg.nl, 0))
    return pl.pallas_call(
        functools.partial(_mixer1_kernel, ns=ns, tl=tl, nl=tg.nl, skew=skew, has_state=has_state),
        grid=tg.grid,
        in_specs=tg.x_specs(D_MODEL) + [tg.state(state_shape)] * len(states)
        + [tg.const(c.shape) for c in consts_a] + [pos_spec, pos_spec]
        + [tg.const(c.shape) for c in consts_b],
        out_specs=[tg.tile(D_MODEL), tg.state(state_shape)],
        out_shape=[jax.ShapeDtypeStruct(x.shape, F32), jax.ShapeDtypeStruct((bsz,) + state_shape, F32)],
        scratch_shapes=[pltpu.VMEM((m, RET_V), BF16)]
        + tg.bufs([((m, 2 * RET_QK), F32), ((m, RET_V), BF16), ((m, RET_V), F32)]),
        compiler_params=_params(),
        name="mixer1",
    )(*([x] * len(tg.x_specs(D_MODEL))), *states, *consts_a, tabs['cos'], tabs['sin'], *consts_b)


def _ffn(x, p, sbuf, w, layer, *, ns, tl, final_norm):
    bsz, length, _ = x.shape
    tg = _Tiling(bsz, length, ns, tl, False)
    has_state = sbuf is not None
    states = [_header(sbuf)] if has_state else []
    consts = [w['nf'][layer], w['wg'][layer], w['wu'][layer], w['ffn_cw'][layer], w['ffn_cb'][layer],
              w['wd'][layer], w['npl'][layer], w['wpg'][layer], w['wpp'][layer], w['nfin']]
    p_spec = pl.BlockSpec((None, ns, tl, PLE_DIM),
                          lambda t: (layer, tg.body_tile(t) // tg.nl, tg.body_tile(t) % tg.nl, 0))
    x, buf_o = pl.pallas_call(
        functools.partial(_ffn_kernel, ns=ns, tl=tl, nl=tg.nl, final_norm=final_norm, skew=False,
                          has_state=has_state),
        grid=tg.grid,
        in_specs=tg.x_specs(D_MODEL) + [p_spec] + [tg.state((SUBLANES, D_FF))] * len(states)
        + [tg.const(c.shape) for c in consts],
        out_specs=[tg.tile(D_MODEL), tg.state((SUBLANES, D_FF))],
        out_shape=[jax.ShapeDtypeStruct(x.shape, F32), jax.ShapeDtypeStruct((bsz, SUBLANES, D_FF), F32)],
        scratch_shapes=[pltpu.VMEM((ns, D_FF // LANES, SUBLANES + tl, LANES), F32)],
        compiler_params=_params(),
        name="ffn%d" % layer,
    )(x, p, *states, *consts)
    return x, buf_o[:, SUBLANES - (FFN_CONV - 1):]


def _retention_tables(ns, tl, pos0, length):
    half = RET_DK // 2
    inv = ROPE_BASE ** (-np.arange(half, dtype=np.float64) / half)
    ang = (pos0 + np.arange(length, dtype=np.float64))[:, None] * inv[None]
    cos, sin = np.cos(ang), np.sin(ang)
    lg = np.log(1.0 - 2.0 ** (-5.0 - np.arange(RET_HEADS, dtype=np.float64)))
    m = ns * tl
    t = np.arange(m) % tl
    same_seq = (np.arange(m)[:, None] // tl) == (np.arange(m)[None, :] // tl)
    diff = (t[:, None] - t[None, :]).astype(np.float64)
    dmat = np.where((diff[None] >= 0) & same_seq[None], np.exp(lg[:, None, None] * np.maximum(diff, 0.0)[None]), 0.0)
    qdec = np.exp(lg[None, :] * (t[:, None] + 1.0))
    kdec = np.exp(lg[None, :] * (tl - 1.0 - t)[:, None])
    cdec = np.exp(lg * tl)
    tabs = {
        'cos': np.concatenate([cos, cos], axis=-1),
        'sin': np.concatenate([-sin, sin], axis=-1),
        'dmat': dmat,
        'qdec': np.repeat(qdec, RET_DK, axis=1),
        'kdec': np.repeat(kdec, RET_DK, axis=1),
        'cdec': np.repeat(cdec, RET_DV)[None],
    }
    return {k: jnp.asarray(v, F32) for k, v in tabs.items()}


def _header(buf):
    return jnp.pad(buf, ((0, 0), (SUBLANES - buf.shape[1], 0), (0, 0)))


def _trunk(x, p, sconv, sssm, sret, sffn, pos0, w, *, ns, ns_ret, tl, emit_gmv, skew):
    x, conv_o, ssm_o, gmv = _mixer0(x, sconv, sssm, w, ns=ns, tl=tl, emit_gmv=emit_gmv, skew=skew)
    x, ffn0 = _ffn(x, p, None if sffn is None else sffn[0], w, 0, ns=ns, tl=tl, final_norm=False)
    x, ret_o = _mixer1(x, sret, w, pos0, ns=ns_ret, tl=tl, skew=skew)
    x, ffn1 = _ffn(x, p, None if sffn is None else sffn[1], w, 1, ns=ns, tl=tl, final_norm=True)
    return x, gmv, conv_o, ssm_o, ret_o, jnp.stack([ffn0, ffn1])


def _prepare_weights(norm_mix, norm_ffn, norm_ple, norm_final, l0_w_in, gm_v_gain, gm_w_s, gm_b_s,
                     ssm_conv_w, ssm_conv_b, ssm_dt_bias, ssm_a_log, ssm_d, ssm_norm, l0_w_out,
                     l1_w_in, ret_norm, l1_w_out, ffn_w_gate, ffn_w_up, ffn_conv_w, ffn_conv_b,
                     ffn_w_down, ple_w_gate, ple_w_proj):
    gm = GM_GROUPS * GM_GDIM
    rep = lambda v: jnp.repeat(v, SSM_HEAD_DIM)[None].astype(F32)
    tile = lambda v: jnp.tile(v, DT_LANES // SSM_HEADS)[None].astype(F32)
    lane = jnp.arange(DT_LANES)
    return {
        'nm0': norm_mix[0][None], 'nm1': norm_mix[1][None],
        'wuvz': l0_w_in[:, :2 * gm + SSM_DIM].astype(BF16),
        'wxbc': l0_w_in[:, 2 * gm + SSM_DIM:2 * gm + SSM_DIM + SSM_CONV_DIM].astype(BF16),
        'wdt': jnp.tile(l0_w_in[:, 2 * gm + SSM_DIM + SSM_CONV_DIM:], (1, DT_LANES // SSM_HEADS)).astype(BF16),
        'dtb': tile(ssm_dt_bias), 'alog': tile(ssm_a_log), 'dskip': rep(ssm_d),
        'expd': ((lane[:, None] < 3 * SSM_HEADS)
                 & (lane[:, None] % SSM_HEADS == jnp.arange(SSM_DIM)[None] // SSM_HEAD_DIM)).astype(BF16),
        'ssm_cw': ssm_conv_w, 'ssm_cb': ssm_conv_b[None], 'snorm': ssm_norm[None], 'vgain': gm_v_gain[None],
        'gmw': gm_w_s, 'gmb': jnp.repeat(gm_b_s.T, GM_GDIM, axis=1),
        'wout0': l0_w_out.astype(BF16),
        'win1': l1_w_in.astype(BF16), 'rnorm': ret_norm[None], 'wout1': l1_w_out.astype(BF16),
        'nf': norm_ffn[:, None], 'wg': ffn_w_gate.astype(BF16), 'wu': ffn_w_up.astype(BF16),
        'ffn_cw': ffn_conv_w, 'ffn_cb': ffn_conv_b[:, None], 'wd': ffn_w_down.astype(BF16),
        'npl': norm_ple[:, None], 'wpg': ple_w_gate.astype(BF16), 'wpp': ple_w_proj.astype(BF16),
        'nfin': norm_final[None],
    }


def kernel(x_prompt, x_sample, state_ssm_conv, state_ssm, state_ret, state_ffn_conv, p_prompt, p_sample, norm_mix, norm_ffn, norm_ple, norm_final, l0_w_in, gm_v_gain, gm_w_s, gm_b_s, ssm_conv_w, ssm_conv_b, ssm_dt_bias, ssm_a_log, ssm_d, ssm_norm, l0_w_out, l1_w_in, ret_norm, l1_w_out, ffn_w_gate, ffn_w_up, ffn_conv_w, ffn_conv_b, ffn_w_down, ple_w_gate, ple_w_proj):
    w = _prepare_weights(norm_mix, norm_ffn, norm_ple, norm_final, l0_w_in, gm_v_gain, gm_w_s, gm_b_s,
                         ssm_conv_w, ssm_conv_b, ssm_dt_bias, ssm_a_log, ssm_d, ssm_norm, l0_w_out,
                         l1_w_in, ret_norm, l1_w_out, ffn_w_gate, ffn_w_up, ffn_conv_w, ffn_conv_b,
                         ffn_w_down, ple_w_gate, ple_w_proj)
    y_p, _, conv_p, ssm_p, ret_p, ffn_p = _trunk(
        x_prompt, p_prompt, None, None, None, None, 0, w, ns=1, ns_ret=1, tl=256, emit_gmv=False, skew=True)
    y_s, gmv_s, conv_s, ssm_s, ret_s, ffn_s = _trunk(
        x_sample, p_sample, state_ssm_conv, state_ssm, state_ret, state_ffn_conv, PAST_LEN, w,
        ns=8, ns_ret=4, tl=32, emit_gmv=True, skew=False)
    return (y_p, y_s, conv_p, conv_s, ssm_p, ssm_s, ret_p, ret_s, ffn_p, ffn_s, gmv_s)
```

```python
import functools

import jax
import jax.numpy as jnp
import numpy as np
from jax import lax
from jax.experimental import pallas as pl
from jax.experimental.pallas import tpu as pltpu

F32 = jnp.float32
BF16 = jnp.bfloat16

D_MODEL = 1024
DEPTH = 2
PAST_LEN = 1024
CHUNK = 64
PLE_DIM = 256
EPS = 1e-6
GM_CHUNK = 128
GM_GROUPS = 8
GM_GDIM = 128
SSM_HEADS = 16
SSM_HEAD_DIM = 64
SSM_DIM = 1024
SSM_GROUPS = 2
SSM_STATE = 128
SSM_CONV = 4
SSM_CONV_DIM = 1536
RET_HEADS = 8
RET_DK = 128
RET_DV = 256
RET_QK = 1024
RET_V = 2048
ROPE_BASE = 10000.0
D_FF = 2816
FFN_CONV = 3

SUBLANES = 8
LANES = 128
GROUP_LANES = SSM_HEADS // SSM_GROUPS * SSM_HEAD_DIM
DT_LANES = 128
VMEM_LIMIT = 56 * 1024 * 1024


def _dot(a, b):
    return jnp.dot(a, b, preferred_element_type=F32)


def _dot_nt(a, b):
    return lax.dot_general(a, b, (((1,), (1,)), ((), ())), preferred_element_type=F32)


def _dot_tn(a, b):
    return lax.dot_general(a, b, (((0,), (0,)), ((), ())), preferred_element_type=F32)


def _rms(x, g):
    return x * lax.rsqrt(jnp.mean(x * x, axis=-1, keepdims=True) + EPS) * g


def _silu(x):
    return x * jax.nn.sigmoid(x)


def _softplus(x):
    return jnp.maximum(x, 0.0) + jnp.log1p(jnp.exp(-jnp.abs(x)))


def _pad_store(pad_ref, start, value):
    rows = value.shape[1]
    for j in range(pad_ref.shape[1]):
        pad_ref[:, j, start:start + rows, :] = value[:, :, j * LANES:(j + 1) * LANES]


def _pad_load(pad_ref, start, rows, j0=0, j1=None):
    j1 = pad_ref.shape[1] if j1 is None else j1
    return jnp.concatenate([pad_ref[:, j, start:start + rows, :] for j in range(j0, j1)], axis=-1)


def _causal_conv(pad_ref, w_ref, b_ref, tl, j0=0, j1=None):
    j1 = pad_ref.shape[1] if j1 is None else j1
    width = w_ref.shape[0]
    lanes = slice(j0 * LANES, j1 * LANES)
    y = b_ref[:, lanes][None]
    for k in range(width):
        y = y + w_ref[k:k + 1, lanes][None] * _pad_load(pad_ref, SUBLANES - (width - 1) + k, tl, j0, j1)
    return y


class _Cols:
    def __init__(self, src):
        self.src = src

    def cols(self, a, b):
        return self.src[:, a:b]


PROJ_BLOCK = 512


def _proj_spec(outputs):
    spec = []
    for o, (w_ref, first, n_cols, _) in enumerate(outputs):
        for c in range(0, n_cols, PROJ_BLOCK):
            spec.append((o, w_ref, first + c, c, min(PROJ_BLOCK, n_cols - c)))
    return spec


def _skewed(t, norm_fn, outputs, body_fn, init_fn, x_refs, bufs, nl, final_fn=None):
    spec = _proj_spec(outputs)
    if bufs is None:
        (x_ref,) = x_refs

        @pl.when(t % nl == 0)
        def _():
            init_fn()

        x = x_ref[...].reshape(-1, D_MODEL)
        h = norm_fn(x)
        parts = [[] for _ in outputs]
        for o, w_ref, wc, _, n in spec:
            parts[o].append(_dot(h, w_ref[:, wc:wc + n]).astype(outputs[o][3]))
        body_fn(x, tuple(_Cols(jnp.concatenate(p, axis=1) if len(p) > 1 else p[0]) for p in parts), lambda n=1: None)
        if final_fn is not None:
            @pl.when(t % nl == nl - 1)
            def _():
                final_fn()
        return

    xa_ref, xb_ref = x_refs

    @pl.when(t == 0)
    def _():
        for r in bufs[1]:
            r[...] = jnp.zeros(r.shape, r.dtype)

    @pl.when(jnp.maximum(t - 1, 0) % nl == 0)
    def _():
        init_fn()

    for parity in range(2):
        @pl.when(t % 2 == parity)
        def _(parity=parity):
            h = norm_fn(xa_ref[...].reshape(-1, D_MODEL))
            pieces = iter(spec)

            def tick(n=1):
                for _ in range(n):
                    piece = next(pieces, None)
                    if piece is not None:
                        o, w_ref, wc, oc, width = piece
                        bufs[parity][o][:, oc:oc + width] = _dot(h, w_ref[:, wc:wc + width]).astype(outputs[o][3])

            body_fn(xb_ref[...].reshape(-1, D_MODEL), tuple(_Cols(r) for r in bufs[1 - parity]), tick)
            tick(len(spec))

    if final_fn is not None:
        @pl.when(jnp.maximum(t - 1, 0) % nl == nl - 1)
        def _():
            final_fn()


def _split_refs(refs, n_x, n_in, n_out, n_scratch):
    refs = list(refs)
    x_refs, refs = refs[:n_x], refs[n_x:]
    ins, refs = refs[:n_in], refs[n_in:]
    outs, refs = refs[:n_out], refs[n_out:]
    scratch, refs = refs[:n_scratch], refs[n_scratch:]
    bufs = (refs[:len(refs) // 2], refs[len(refs) // 2:]) if refs else None
    return x_refs, ins, outs, scratch, bufs


def _mixer0_kernel(*refs, ns, tl, nl, tc, n_gm, emit_gmv, skew, has_state):
    n_state = 2 if has_state else 0
    x_refs, ins, outs, scratch, bufs = _split_refs(refs, 2 if skew else 1, n_state + 15, 4 if emit_gmv else 3, 6)
    sconv_ref, sssm_ref = ins[:n_state] if has_state else (None, None)
    (nm_ref, wuvz_ref, wxbc_ref, wdt_ref, dtb_ref, alog_ref, expd_ref, dskip_ref, cw_ref,
     cb_ref, snorm_ref, vgain_ref, gmw_ref, gmb_ref, wout_ref) = ins[n_state:]
    xo_ref, convo_ref, sso_ref = outs[:3]
    gmv_ref = outs[3] if emit_gmv else None
    xpad_ref, xs_ref, cs_ref, y_ref, mix_ref, ht_ref = scratch
    pairs_per_group = SSM_HEADS // SSM_GROUPS // 2
    m = ns * tl
    nck = tl // tc
    nchunks = m // tc
    gm = GM_GROUPS * GM_GDIM

    def init():
        if not has_state:
            for j in range(xpad_ref.shape[1]):
                xpad_ref[:, j, 0:SUBLANES, :] = jnp.zeros((ns, SUBLANES, LANES), F32)
            ht_ref[...] = jnp.zeros(ht_ref.shape, F32)
            return
        _pad_store(xpad_ref, 0, sconv_ref[...])
        for s in range(ns):
            for pr in range(SSM_HEADS // 2):
                g, q = divmod(pr, pairs_per_group)
                ht_ref[s, g, :, q * 2 * SSM_HEAD_DIM:(q + 1) * 2 * SSM_HEAD_DIM] = sssm_ref[s, pr].T

    def final():
        for s in range(ns):
            for pr in range(SSM_HEADS // 2):
                g, q = divmod(pr, pairs_per_group)
                sso_ref[s, pr] = ht_ref[s, g, :, q * 2 * SSM_HEAD_DIM:(q + 1) * 2 * SSM_HEAD_DIM].T

    def norm(x):
        return _rms(x, nm_ref[...]).astype(BF16)

    outputs = [(wuvz_ref, 0, 2 * gm + SSM_DIM, F32), (wxbc_ref, 0, SSM_CONV_DIM, F32), (wdt_ref, 0, DT_LANES, F32)]

    def body(x, projected, tick):
        uvz, xbc, dtr = projected

        ri = lax.broadcasted_iota(jnp.int32, (n_gm, n_gm), 0)
        ci = lax.broadcasted_iota(jnp.int32, (n_gm, n_gm), 1)
        gm_mask = (ci // CHUNK) <= (ri // CHUNK)
        for g in range(GM_GROUPS):
            lanes = slice(g * GM_GDIM, (g + 1) * GM_GDIM)
            u_g = jax.nn.gelu(uvz.cols(g * GM_GDIM, (g + 1) * GM_GDIM))
            v_g = jax.nn.gelu(uvz.cols(gm + g * GM_GDIM, gm + (g + 1) * GM_GDIM))
            v_g = v_g * lax.rsqrt(jnp.mean(v_g * v_g, axis=-1, keepdims=True) + EPS) * vgain_ref[:, lanes]
            if emit_gmv:
                gmv_ref[:, :, lanes] = v_g.reshape(ns, tl, GM_GDIM)
            w_g = jnp.where(gm_mask, gmw_ref[g], 0.0).astype(BF16)
            v_gb = v_g.astype(BF16)
            for c in range(m // n_gm):
                rows = slice(c * n_gm, (c + 1) * n_gm)
                s = _dot(w_g, v_gb[rows]) + gmb_ref[:, lanes]
                mix_ref[rows, lanes] = (u_g[rows] * s).astype(BF16)
            if g % 2 == 1:
                tick()
        x = x + _dot(mix_ref[:, :gm], wout_ref[:gm, :])

        for j in range(SSM_CONV_DIM // LANES):
            xpad_ref[:, j, SUBLANES:, :] = xbc.cols(j * LANES, (j + 1) * LANES).reshape(ns, tl, LANES)
        xc = _silu(_causal_conv(xpad_ref, cw_ref, cb_ref, tl)).reshape(m, SSM_CONV_DIM)
        tail = _pad_load(xpad_ref, tl, SUBLANES)
        _pad_store(xpad_ref, 0, tail)
        convo_ref[...] = tail
        tick()
        xh = xc[:, :SSM_DIM]
        bm = xc[:, SSM_DIM:SSM_DIM + SSM_GROUPS * SSM_STATE].astype(BF16)
        cm = xc[:, SSM_DIM + SSM_GROUPS * SSM_STATE:].astype(BF16)

        dt_n = _softplus(dtr.cols(0, DT_LANES) + dtb_ref[...])
        da_n = dt_n * (-jnp.exp(alog_ref[...]))

        def split3(v):
            hi = v.astype(BF16).astype(F32)
            r1 = v - hi
            mid = r1.astype(BF16).astype(F32)
            return hi, mid, r1 - mid

        lane_grp = lax.broadcasted_iota(jnp.int32, (m, DT_LANES), 1) // SSM_HEADS

        def expand(v):
            hi, mid, lo = split3(v)
            parts = jnp.where(lane_grp == 0, hi, jnp.where(lane_grp == 1, mid, lo))
            return _dot(parts.astype(BF16), expd_ref[...])

        ri = lax.broadcasted_iota(jnp.int32, (m, m), 0)
        ci = lax.broadcasted_iota(jnp.int32, (m, m), 1)
        tri = jnp.where(((ri // tc) == (ci // tc)) & (ci <= ri), 1.0, 0.0).astype(BF16)
        cs3 = _dot(tri, jnp.concatenate(split3(da_n), axis=1).astype(BF16))
        cs_n = cs3[:, :DT_LANES] + cs3[:, DT_LANES:2 * DT_LANES] + cs3[:, 2 * DT_LANES:]
        cs_ref[...] = expand(cs_n)
        dtx = expand(dt_n)
        xs_ref[...] = xh * dtx
        y_ref[...] = dskip_ref[...] * xh
        tick()

        def dup_rows(b_g):
            parts = []
            for c in range(nchunks):
                blk = b_g[c * tc:(c + 1) * tc]
                if tc < CHUNK:
                    blk = jnp.concatenate([blk, jnp.zeros((CHUNK - tc, SSM_STATE), BF16)], axis=0)
                parts += [blk, blk]
            return jnp.concatenate(parts, axis=0)

        cb2 = [_dot_nt(cm[:, g * SSM_STATE:(g + 1) * SSM_STATE], dup_rows(bm[:, g * SSM_STATE:(g + 1) * SSM_STATE]))
               for g in range(SSM_GROUPS)]

        t_i = lax.broadcasted_iota(jnp.int32, (tc, SSM_DIM), 0)
        s_i = lax.broadcasted_iota(jnp.int32, (tc, SSM_DIM), 1) % SSM_HEAD_DIM
        causal = s_i <= t_i
        diag = s_i == t_i
        quad_head = lax.broadcasted_iota(jnp.int32, (tc, 4 * SSM_HEAD_DIM), 1) // SSM_HEAD_DIM
        heads_per_group = SSM_HEADS // SSM_GROUPS

        for c in range(nchunks):
            seq = c // nck
            rows = slice(c * tc, (c + 1) * tc)
            cs_c = cs_ref[rows, :]
            cs_row = jnp.sum(jnp.where(diag, cs_c, 0.0), axis=0, keepdims=True)
            cs_last = cs_c[tc - 1:tc, :]
            lmat = jnp.exp(jnp.where(causal, cs_c - cs_row, -1e30))
            cb_c = jnp.concatenate(
                [cb2[g][rows, c * 2 * CHUNK:(c + 1) * 2 * CHUNK] for g in range(SSM_GROUPS)
                 for _ in range(heads_per_group // 2)], axis=1)
            mm = (lmat * cb_c).astype(BF16)
            xs_c = xs_ref[rows, :]
            y_parts = []
            for q in range(SSM_HEADS // 4):
                ql = slice(q * 4 * SSM_HEAD_DIM, (q + 1) * 4 * SSM_HEAD_DIM)
                xq = xs_c[:, ql]
                blocks = []
                for hh in range(4):
                    blk = jnp.where(quad_head == hh, xq, 0.0).astype(BF16)
                    if tc < CHUNK:
                        blk = jnp.concatenate([blk, jnp.zeros((CHUNK - tc, 4 * SSM_HEAD_DIM), BF16)], axis=0)
                    blocks.append(blk)
                y_parts.append(_dot(mm[:, ql], jnp.concatenate(blocks, axis=0)))
            y_c = jnp.concatenate(y_parts, axis=1)
            ecs = jnp.exp(cs_c)
            xsd = (xs_c * jnp.exp(cs_last - cs_c)).astype(BF16)
            e_last = ecs[tc - 1:tc, :]
            y_off = []
            for g in range(SSM_GROUPS):
                gl = slice(g * GROUP_LANES, (g + 1) * GROUP_LANES)
                sl = slice(g * SSM_STATE, (g + 1) * SSM_STATE)
                h_prev = ht_ref[seq, g]
                y_off.append(_dot(cm[rows, sl], h_prev.astype(BF16)))
                ht_ref[seq, g] = h_prev * e_last[:, gl] + _dot_tn(bm[rows, sl], xsd[:, gl])
            y_ref[rows, :] = y_ref[rows, :] + y_c + jnp.concatenate(y_off, axis=1) * ecs
            tick()

        yb = _rms(y_ref[...] * _silu(uvz.cols(2 * gm, 2 * gm + SSM_DIM)), snorm_ref[...])
        xo_ref[...] = (x + _dot(yb.astype(BF16), wout_ref[gm:, :])).reshape(ns, tl, D_MODEL)

    _skewed(pl.program_id(0), norm, outputs, body, init, x_refs, bufs, nl, final)


def _mixer1_kernel(*refs, ns, tl, nl, skew, has_state):
    n_state = 1 if has_state else 0
    x_refs, ins, outs, scratch, bufs = _split_refs(refs, 2 if skew else 1, n_state + 10, 2, 1)
    sret_ref = ins[0] if has_state else None
    (nm_ref, win_ref, cos_ref, sin_ref, dmat_ref, qdec_ref, kdec_ref, cdec_ref, rnorm_ref,
     wout_ref) = ins[n_state:]
    xo_ref, s_ref = outs
    (mix_ref,) = scratch
    m = ns * tl

    def init():
        s_ref[...] = sret_ref[...] if has_state else jnp.zeros(s_ref.shape, F32)

    def norm(x):
        return _rms(x, nm_ref[...]).astype(BF16)

    outputs = [(win_ref, 0, 2 * RET_QK, F32), (win_ref, 2 * RET_QK, RET_V, BF16), (win_ref, 2 * RET_QK + RET_V, RET_V, F32)]

    def body(x, projected, tick):
        qk, vv, gg = projected
        cos = jnp.concatenate([cos_ref[...]] * ns, axis=0) if ns > 1 else cos_ref[...]
        sin = jnp.concatenate([sin_ref[...]] * ns, axis=0) if ns > 1 else sin_ref[...]
        qdec = qdec_ref[...]
        kdec = kdec_ref[...]
        scale = RET_DK ** -0.5
        for hd in range(RET_HEADS):
            kl = slice(hd * RET_DK, (hd + 1) * RET_DK)
            vl = slice(hd * RET_DV, (hd + 1) * RET_DV)
            q = qk.cols(hd * RET_DK, (hd + 1) * RET_DK)
            k = qk.cols(RET_QK + hd * RET_DK, RET_QK + (hd + 1) * RET_DK)
            v = vv.cols(hd * RET_DV, (hd + 1) * RET_DV)
            gate = gg.cols(hd * RET_DV, (hd + 1) * RET_DV)
            q = q * cos + pltpu.roll(q, RET_DK // 2, axis=1) * sin
            k = (k * cos + pltpu.roll(k, RET_DK // 2, axis=1) * sin) * scale
            sc = _dot_nt(q.astype(BF16), k.astype(BF16)) * dmat_ref[hd]
            y = _dot(sc.astype(BF16), v)
            qd = (q * qdec[:, kl]).astype(BF16)
            kd = (k * kdec[:, kl]).astype(BF16)
            y_inter = []
            for s in range(ns):
                rows = slice(s * tl, (s + 1) * tl)
                s_prev = s_ref[s, hd]
                y_inter.append(_dot(qd[rows], s_prev.astype(BF16)))
                s_ref[s, hd] = s_prev * cdec_ref[:, vl] + _dot_tn(kd[rows], v[rows])
            y = y + (jnp.concatenate(y_inter, axis=0) if ns > 1 else y_inter[0])
            if hd % 2 == 0:
                tick()
            mu = jnp.mean(y, axis=-1, keepdims=True)
            yc = y - mu
            var = jnp.mean(yc * yc, axis=-1, keepdims=True)
            yn = yc * lax.rsqrt(var + EPS) * rnorm_ref[:, vl]
            mix_ref[:, vl] = (_silu(gate) * yn).astype(BF16)
            tick()
            if hd % 2 == 1:
                pair = slice((hd - 1) * RET_DV, (hd + 1) * RET_DV)
                x = x + _dot(mix_ref[:, pair], wout_ref[pair, :])
        xo_ref[...] = x.reshape(ns, tl, D_MODEL)

    _skewed(pl.program_id(0), norm, outputs, body, init, x_refs, bufs, nl)


def _ffn_kernel(*refs, ns, tl, nl, final_norm, skew, has_state):
    n_state = 1 if has_state else 0
    x_refs, ins, outs, scratch, bufs = _split_refs(refs, 2 if skew else 1, n_state + 11, 2, 1)
    p_ref = ins[0]
    sbuf_ref = ins[1] if has_state else None
    (nf_ref, wg_ref, wu_ref, cw_ref, cb_ref, wd_ref, npl_ref, wpg_ref, wpp_ref, nfin_ref) = ins[1 + n_state:]
    xo_ref, bufo_ref = outs
    (gpad_ref,) = scratch
    m = ns * tl

    def init():
        if has_state:
            _pad_store(gpad_ref, 0, sbuf_ref[...])
        else:
            for j in range(gpad_ref.shape[1]):
                gpad_ref[:, j, 0:SUBLANES, :] = jnp.zeros((ns, SUBLANES, LANES), F32)

    def norm(x):
        return _rms(x, nf_ref[...]).astype(BF16)

    outputs = []

    def body(x, projected, tick):
        pp = _dot(p_ref[...].reshape(m, PLE_DIM).astype(BF16), wpp_ref[...])
        h = norm(x)
        _pad_store(gpad_ref, SUBLANES, _dot(h, wg_ref[...]).reshape(ns, tl, D_FF))
        gate = _causal_conv(gpad_ref, cw_ref, cb_ref, tl).reshape(m, D_FF)
        tail = _pad_load(gpad_ref, tl, SUBLANES)
        _pad_store(gpad_ref, 0, tail)
        bufo_ref[...] = tail
        act = (_silu(gate) * _dot(h, wu_ref[...])).astype(BF16)
        x = x + _dot(act, wd_ref[...])
        pgate = jax.nn.sigmoid(_dot(_rms(x, npl_ref[...]).astype(BF16), wpg_ref[...]))
        x = x + pgate * pp
        if final_norm:
            x = _rms(x, nfin_ref[...])
        xo_ref[...] = x.reshape(ns, tl, D_MODEL)

    _skewed(pl.program_id(0), norm, outputs, body, init, x_refs, bufs, nl)


class _Tiling:
    def __init__(self, bsz, length, ns, tl, skew):
        self.ns, self.tl, self.skew = ns, tl, skew
        self.nl = length // tl
        self.n_tiles = (bsz // ns) * self.nl
        self.grid = (self.n_tiles + 1,) if skew else (self.n_tiles,)

    def body_tile(self, t):
        return jnp.maximum(t - 1, 0) if self.skew else t

    def proj_tile(self, t):
        return jnp.minimum(t, self.n_tiles - 1)

    def x_specs(self, width):
        blk = (self.ns, self.tl, width)
        body = pl.BlockSpec(blk, lambda t: (self.body_tile(t) // self.nl, self.body_tile(t) % self.nl, 0))
        if not self.skew:
            return [body]
        return [pl.BlockSpec(blk, lambda t: (self.proj_tile(t) // self.nl, self.proj_tile(t) % self.nl, 0)), body]

    def tile(self, width):
        return self.x_specs(width)[-1]

    def state(self, tail):
        zeros = (0,) * len(tail)
        return pl.BlockSpec((self.ns,) + tail, lambda t: (self.body_tile(t) // self.nl,) + zeros)

    def const(self, shape, block=None):
        if block is not None:
            shape, index = block
            return pl.BlockSpec(shape, lambda t: index, pipeline_mode=pl.Buffered(1))
        zeros = (0,) * len(shape)
        return pl.BlockSpec(shape, lambda t: zeros, pipeline_mode=pl.Buffered(1))

    def layer(self, arr, layer):
        return self.const(None, ((None,) + arr.shape[1:], (layer,) + (0,) * (arr.ndim - 1)))

    def bufs(self, shapes):
        return [pltpu.VMEM(s, d) for _ in range(2) for s, d in shapes] if self.skew else []


def _params():
    return pltpu.CompilerParams(dimension_semantics=("arbitrary",), vmem_limit_bytes=VMEM_LIMIT)


def _mixer0(x, sconv, sssm, w, *, ns, tl, emit_gmv, skew):
    bsz, length, _ = x.shape
    tg = _Tiling(bsz, length, ns, tl, skew)
    tc = min(tl, CHUNK)
    n_gm = min(tl, GM_CHUNK)
    m = ns * tl
    has_state = sconv is not None
    pair_shape = (SSM_HEADS // 2, 2 * SSM_HEAD_DIM, SSM_STATE)
    states = [_header(sconv), sssm.reshape((bsz,) + pair_shape)] if has_state else []
    state_specs = [tg.state((SUBLANES, SSM_CONV_DIM)), tg.state(pair_shape)]
    gm2 = 2 * GM_GROUPS * GM_GDIM + SSM_DIM
    assert gm2 % SSM_CONV_DIM == 0
    win_specs = [tg.const(w['nm0'].shape),
                 tg.const(None, ((D_MODEL, gm2), (0, 0))),
                 tg.const(None, ((D_MODEL, SSM_CONV_DIM), (0, gm2 // SSM_CONV_DIM)))]
    consts = [w['nm0'], w['win0'], w['win0'], w['wdt'], w['dtb'], w['alog'], w['expd'], w['dskip'], w['ssm_cw'],
              w['ssm_cb'], w['snorm'], w['vgain'], w['gmw'][:, :n_gm, :n_gm], w['gmb'][:n_gm], w['wout0']]
    out_shape = [jax.ShapeDtypeStruct(x.shape, F32),
                 jax.ShapeDtypeStruct((bsz, SUBLANES, SSM_CONV_DIM), F32),
                 jax.ShapeDtypeStruct((bsz,) + pair_shape, F32)]
    out_specs = [tg.tile(D_MODEL)] + state_specs
    if emit_gmv:
        out_shape.append(jax.ShapeDtypeStruct((bsz, length, GM_GROUPS * GM_GDIM), F32))
        out_specs.append(tg.tile(GM_GROUPS * GM_GDIM))
    outs = pl.pallas_call(
        functools.partial(_mixer0_kernel, ns=ns, tl=tl, nl=tg.nl, tc=tc, n_gm=n_gm, emit_gmv=emit_gmv, skew=skew,
                          has_state=has_state),
        grid=tg.grid,
        in_specs=tg.x_specs(D_MODEL) + state_specs[:len(states)] + win_specs
        + [tg.const(c.shape) for c in consts[len(win_specs):]],
        out_specs=out_specs,
        out_shape=out_shape,
        scratch_shapes=[pltpu.VMEM((ns, SSM_CONV_DIM // LANES, SUBLANES + tl, LANES), F32),
                        pltpu.VMEM((m, SSM_DIM), F32), pltpu.VMEM((m, SSM_DIM), F32),
                        pltpu.VMEM((m, SSM_DIM), F32), pltpu.VMEM((m, GM_GROUPS * GM_GDIM), BF16),
                        pltpu.VMEM((ns, SSM_GROUPS, SSM_STATE, GROUP_LANES), F32)]
        + tg.bufs([((m, 2 * GM_GROUPS * GM_GDIM + SSM_DIM), F32), ((m, SSM_CONV_DIM), F32), ((m, DT_LANES), F32)]),
        compiler_params=_params(),
        name="mixer0",
    )(*([x] * len(tg.x_specs(D_MODEL))), *states, *consts)
    conv_o = outs[1][:, SUBLANES - (SSM_CONV - 1):]
    ssm_o = outs[2].reshape(bsz, SSM_HEADS, SSM_HEAD_DIM, SSM_STATE)
    return outs[0], conv_o, ssm_o, (outs[3] if emit_gmv else None)


def _mixer1(x, sret, w, pos0, *, ns, tl, skew):
    bsz, length, _ = x.shape
    tg = _Tiling(bsz, length, ns, tl, skew)
    m = ns * tl
    tabs = _retention_tables(ns, tl, pos0, length)
    has_state = sret is not None
    state_shape = (RET_HEADS, RET_DK, RET_DV)
    states = [sret] if has_state else []
    consts_a = [w['nm1'], w['win1']]
    consts_b = [tabs['dmat'], tabs['qdec'], tabs['kdec'], tabs['cdec'], w['rnorm'], w['wout1']]
    pos_spec = pl.BlockSpec((tl, RET_DK), lambda t: (tg.body_tile(t) % tg.nl, 0))
    return pl.pallas_call(
        functools.partial(_mixer1_kernel, ns=ns, tl=tl, nl=tg.nl, skew=skew, has_state=has_state),
        grid=tg.grid,
        in_specs=tg.x_specs(D_MODEL) + [tg.state(state_shape)] * len(states)
        + [tg.const(c.shape) for c in consts_a] + [pos_spec, pos_spec]
        + [tg.const(c.shape) for c in consts_b],
        out_specs=[tg.tile(D_MODEL), tg.state(state_shape)],
        out_shape=[jax.ShapeDtypeStruct(x.shape, F32), jax.ShapeDtypeStruct((bsz,) + state_shape, F32)],
        scratch_shapes=[pltpu.VMEM((m, RET_V), BF16)]
        + tg.bufs([((m, 2 * RET_QK), F32), ((m, RET_V), BF16), ((m, RET_V), F32)]),
        compiler_params=_params(),
        name="mixer1",
    )(*([x] * len(tg.x_specs(D_MODEL))), *states, *consts_a, tabs['cos'], tabs['sin'], *consts_b)


def _ffn(x, p, sbuf, w, layer, *, ns, tl, final_norm):
    bsz, length, _ = x.shape
    tg = _Tiling(bsz, length, ns, tl, False)
    has_state = sbuf is not None
    states = [_header(sbuf)] if has_state else []
    stacked = [w['nf'], w['wg'], w['wu'], w['ffn_cw'], w['ffn_cb'], w['wd'], w['npl'], w['wpg'], w['wpp']]
    p_spec = pl.BlockSpec((None, ns, tl, PLE_DIM),
                          lambda t: (layer, tg.body_tile(t) // tg.nl, tg.body_tile(t) % tg.nl, 0))
    x, buf_o = pl.pallas_call(
        functools.partial(_ffn_kernel, ns=ns, tl=tl, nl=tg.nl, final_norm=final_norm, skew=False,
                          has_state=has_state),
        grid=tg.grid,
        in_specs=tg.x_specs(D_MODEL) + [p_spec] + [tg.state((SUBLANES, D_FF))] * len(states)
        + [tg.layer(c, layer) for c in stacked] + [tg.const(w['nfin'].shape)],
        out_specs=[tg.tile(D_MODEL), tg.state((SUBLANES, D_FF))],
        out_shape=[jax.ShapeDtypeStruct(x.shape, F32), jax.ShapeDtypeStruct((bsz, SUBLANES, D_FF), F32)],
        scratch_shapes=[pltpu.VMEM((ns, D_FF // LANES, SUBLANES + tl, LANES), F32)],
        compiler_params=_params(),
        name="ffn%d" % layer,
    )(x, p, *states, *stacked, w['nfin'])
    return x, buf_o[:, SUBLANES - (FFN_CONV - 1):]


def _retention_tables(ns, tl, pos0, length):
    half = RET_DK // 2
    inv = ROPE_BASE ** (-np.arange(half, dtype=np.float64) / half)
    ang = (pos0 + np.arange(length, dtype=np.float64))[:, None] * inv[None]
    cos, sin = np.cos(ang), np.sin(ang)
    lg = np.log(1.0 - 2.0 ** (-5.0 - np.arange(RET_HEADS, dtype=np.float64)))
    m = ns * tl
    t = np.arange(m) % tl
    same_seq = (np.arange(m)[:, None] // tl) == (np.arange(m)[None, :] // tl)
    diff = (t[:, None] - t[None, :]).astype(np.float64)
    dmat = np.where((diff[None] >= 0) & same_seq[None], np.exp(lg[:, None, None] * np.maximum(diff, 0.0)[None]), 0.0)
    qdec = np.exp(lg[None, :] * (t[:, None] + 1.0))
    kdec = np.exp(lg[None, :] * (tl - 1.0 - t)[:, None])
    cdec = np.exp(lg * tl)
    tabs = {
        'cos': np.concatenate([cos, cos], axis=-1),
        'sin': np.concatenate([-sin, sin], axis=-1),
        'dmat': dmat,
        'qdec': np.repeat(qdec, RET_DK, axis=1),
        'kdec': np.repeat(kdec, RET_DK, axis=1),
        'cdec': np.repeat(cdec, RET_DV)[None],
    }
    return {k: jnp.asarray(v, F32) for k, v in tabs.items()}


def _header(buf):
    return jnp.pad(buf, ((0, 0), (SUBLANES - buf.shape[1], 0), (0, 0)))


def _trunk(x, p, sconv, sssm, sret, sffn, pos0, w, *, ns, ns_ret, tl, tl_ffn, emit_gmv, skew):
    x, conv_o, ssm_o, gmv = _mixer0(x, sconv, sssm, w, ns=ns, tl=tl, emit_gmv=emit_gmv, skew=skew)
    x, ffn0 = _ffn(x, p, None if sffn is None else sffn[0], w, 0, ns=ns, tl=tl_ffn, final_norm=False)
    x, ret_o = _mixer1(x, sret, w, pos0, ns=ns_ret, tl=tl, skew=skew)
    x, ffn1 = _ffn(x, p, None if sffn is None else sffn[1], w, 1, ns=ns, tl=tl_ffn, final_norm=True)
    return x, gmv, conv_o, ssm_o, ret_o, jnp.stack([ffn0, ffn1])


def _prepare_weights(norm_mix, norm_ffn, norm_ple, norm_final, l0_w_in, gm_v_gain, gm_w_s, gm_b_s,
                     ssm_conv_w, ssm_conv_b, ssm_dt_bias, ssm_a_log, ssm_d, ssm_norm, l0_w_out,
                     l1_w_in, ret_norm, l1_w_out, ffn_w_gate, ffn_w_up, ffn_conv_w, ffn_conv_b,
                     ffn_w_down, ple_w_gate, ple_w_proj):
    gm = GM_GROUPS * GM_GDIM
    rep = lambda v: jnp.repeat(v, SSM_HEAD_DIM)[None].astype(F32)
    tile = lambda v: jnp.tile(v, DT_LANES // SSM_HEADS)[None].astype(F32)
    lane = jnp.arange(DT_LANES)
    return {
        'nm0': norm_mix[0][None], 'nm1': norm_mix[1][None],
        'win0': l0_w_in.astype(BF16),
        'wdt': jnp.tile(l0_w_in[:, 2 * gm + SSM_DIM + SSM_CONV_DIM:], (1, DT_LANES // SSM_HEADS)).astype(BF16),
        'dtb': tile(ssm_dt_bias), 'alog': tile(ssm_a_log), 'dskip': rep(ssm_d),
        'expd': ((lane[:, None] < 3 * SSM_HEADS)
                 & (lane[:, None] % SSM_HEADS == jnp.arange(SSM_DIM)[None] // SSM_HEAD_DIM)).astype(BF16),
        'ssm_cw': ssm_conv_w, 'ssm_cb': ssm_conv_b[None], 'snorm': ssm_norm[None], 'vgain': gm_v_gain[None],
        'gmw': gm_w_s, 'gmb': jnp.repeat(gm_b_s.T, GM_GDIM, axis=1),
        'wout0': l0_w_out.astype(BF16),
        'win1': l1_w_in.astype(BF16), 'rnorm': ret_norm[None], 'wout1': l1_w_out.astype(BF16),
        'nf': norm_ffn[:, None], 'wg': ffn_w_gate.astype(BF16), 'wu': ffn_w_up.astype(BF16),
        'ffn_cw': ffn_conv_w, 'ffn_cb': ffn_conv_b[:, None], 'wd': ffn_w_down.astype(BF16),
        'npl': norm_ple[:, None], 'wpg': ple_w_gate.astype(BF16), 'wpp': ple_w_proj.astype(BF16),
        'nfin': norm_final[None],
    }


def kernel(x_prompt, x_sample, state_ssm_conv, state_ssm, state_ret, state_ffn_conv, p_prompt, p_sample, norm_mix, norm_ffn, norm_ple, norm_final, l0_w_in, gm_v_gain, gm_w_s, gm_b_s, ssm_conv_w, ssm_conv_b, ssm_dt_bias, ssm_a_log, ssm_d, ssm_norm, l0_w_out, l1_w_in, ret_norm, l1_w_out, ffn_w_gate, ffn_w_up, ffn_conv_w, ffn_conv_b, ffn_w_down, ple_w_gate, ple_w_proj):
    w = _prepare_weights(norm_mix, norm_ffn, norm_ple, norm_final, l0_w_in, gm_v_gain, gm_w_s, gm_b_s,
                         ssm_conv_w, ssm_conv_b, ssm_dt_bias, ssm_a_log, ssm_d, ssm_norm, l0_w_out,
                         l1_w_in, ret_norm, l1_w_out, ffn_w_gate, ffn_w_up, ffn_conv_w, ffn_conv_b,
                         ffn_w_down, ple_w_gate, ple_w_proj)
    y_p, _, conv_p, ssm_p, ret_p, ffn_p = _trunk(
        x_prompt, p_prompt, None, None, None, None, 0, w, ns=1, ns_ret=1, tl=256, tl_ffn=512, emit_gmv=False,
        skew=True)
    y_s, gmv_s, conv_s, ssm_s, ret_s, ffn_s = _trunk(
        x_sample, p_sample, state_ssm_conv, state_ssm, state_ret, state_ffn_conv, PAST_LEN, w,
        ns=8, ns_ret=4, tl=32, tl_ffn=32, emit_gmv=True, skew=False)
    return (y_p, y_s, conv_p, conv_s, ssm_p, ssm_s, ret_p, ret_s, ffn_p, ffn_s, gmv_s)
```

```python
import functools

import jax
import jax.numpy as jnp
import numpy as np
from jax import lax
from jax.experimental import pallas as pl
from jax.experimental.pallas import tpu as pltpu

F32 = jnp.float32
BF16 = jnp.bfloat16

D_MODEL = 1024
DEPTH = 2
PAST_LEN = 1024
CHUNK = 64
PLE_DIM = 256
EPS = 1e-6
GM_CHUNK = 128
GM_GROUPS = 8
GM_GDIM = 128
SSM_HEADS = 16
SSM_HEAD_DIM = 64
SSM_DIM = 1024
SSM_GROUPS = 2
SSM_STATE = 128
SSM_CONV = 4
SSM_CONV_DIM = 1536
RET_HEADS = 8
RET_DK = 128
RET_DV = 256
RET_QK = 1024
RET_V = 2048
ROPE_BASE = 10000.0
D_FF = 2816
FFN_CONV = 3

SUBLANES = 8
LANES = 128
GROUP_LANES = SSM_HEADS // SSM_GROUPS * SSM_HEAD_DIM
DT_LANES = 128
VMEM_LIMIT = 56 * 1024 * 1024


def _dot(a, b):
    return jnp.dot(a, b, preferred_element_type=F32)


def _dot_nt(a, b):
    return lax.dot_general(a, b, (((1,), (1,)), ((), ())), preferred_element_type=F32)


def _dot_tn(a, b):
    return lax.dot_general(a, b, (((0,), (0,)), ((), ())), preferred_element_type=F32)


def _rms(x, g):
    return x * lax.rsqrt(jnp.mean(x * x, axis=-1, keepdims=True) + EPS) * g


def _silu(x):
    return x * jax.nn.sigmoid(x)


def _softplus(x):
    return jnp.maximum(x, 0.0) + jnp.log1p(jnp.exp(-jnp.abs(x)))


def _pad_store(pad_ref, start, value):
    rows = value.shape[1]
    for j in range(pad_ref.shape[1]):
        pad_ref[:, j, start:start + rows, :] = value[:, :, j * LANES:(j + 1) * LANES]


def _pad_load(pad_ref, start, rows, j0=0, j1=None):
    j1 = pad_ref.shape[1] if j1 is None else j1
    return jnp.concatenate([pad_ref[:, j, start:start + rows, :] for j in range(j0, j1)], axis=-1)


def _causal_conv(pad_ref, w_ref, b_ref, tl, j0=0, j1=None):
    j1 = pad_ref.shape[1] if j1 is None else j1
    width = w_ref.shape[0]
    lanes = slice(j0 * LANES, j1 * LANES)
    y = b_ref[:, lanes][None]
    for k in range(width):
        y = y + w_ref[k:k + 1, lanes][None] * _pad_load(pad_ref, SUBLANES - (width - 1) + k, tl, j0, j1)
    return y


class _Cols:
    def __init__(self, src):
        self.src = src

    def cols(self, a, b):
        return self.src[:, a:b]


PROJ_BLOCK = 512
JOINT_ROWS = 128


def _proj_spec(outputs):
    spec = []
    for o, (w_ref, first, n_cols, _) in enumerate(outputs):
        for c in range(0, n_cols, PROJ_BLOCK):
            spec.append((o, w_ref, first + c, c, min(PROJ_BLOCK, n_cols - c)))
    return spec


def _skewed(t, norm_fn, outputs, body_fn, init_fn, x_refs, bufs, nl, final_fn=None):
    spec = _proj_spec(outputs)
    if bufs is None:
        (x_ref,) = x_refs

        @pl.when(t % nl == 0)
        def _():
            init_fn()

        x = x_ref[...].reshape(-1, D_MODEL)
        h = norm_fn(x)
        parts = [[] for _ in outputs]
        for o, w_ref, wc, _, n in spec:
            parts[o].append(_dot(h, w_ref[:, wc:wc + n]).astype(outputs[o][3]))
        body_fn(x, tuple(_Cols(jnp.concatenate(p, axis=1) if len(p) > 1 else p[0]) for p in parts), lambda n=1: None)
        if final_fn is not None:
            @pl.when(t % nl == nl - 1)
            def _():
                final_fn()
        return

    xa_ref, xb_ref = x_refs

    @pl.when(t == 0)
    def _():
        for r in bufs[1]:
            r[...] = jnp.zeros(r.shape, r.dtype)

    @pl.when(jnp.maximum(t - 1, 0) % nl == 0)
    def _():
        init_fn()

    for parity in range(2):
        @pl.when(t % 2 == parity)
        def _(parity=parity):
            h = norm_fn(xa_ref[...].reshape(-1, D_MODEL))
            pieces = iter(spec)

            def tick(n=1):
                for _ in range(n):
                    piece = next(pieces, None)
                    if piece is not None:
                        o, w_ref, wc, oc, width = piece
                        bufs[parity][o][:, oc:oc + width] = _dot(h, w_ref[:, wc:wc + width]).astype(outputs[o][3])

            body_fn(xb_ref[...].reshape(-1, D_MODEL), tuple(_Cols(r) for r in bufs[1 - parity]), tick)
            tick(len(spec))

    if final_fn is not None:
        @pl.when(jnp.maximum(t - 1, 0) % nl == nl - 1)
        def _():
            final_fn()


def _split_refs(refs, n_x, n_in, n_out, n_scratch):
    refs = list(refs)
    x_refs, refs = refs[:n_x], refs[n_x:]
    ins, refs = refs[:n_in], refs[n_in:]
    outs, refs = refs[:n_out], refs[n_out:]
    scratch, refs = refs[:n_scratch], refs[n_scratch:]
    bufs = (refs[:len(refs) // 2], refs[len(refs) // 2:]) if refs else None
    return x_refs, ins, outs, scratch, bufs


def _mixer0_kernel(*refs, ns, tl, nl, tc, n_gm, emit_gmv, skew, has_state):
    n_state = 2 if has_state else 0
    x_refs, ins, outs, scratch, bufs = _split_refs(refs, 2 if skew else 1, n_state + 15, 4 if emit_gmv else 3, 6)
    sconv_ref, sssm_ref = ins[:n_state] if has_state else (None, None)
    (nm_ref, wuvz_ref, wxbc_ref, wdt_ref, dtb_ref, alog_ref, expd_ref, dskip_ref, cw_ref,
     cb_ref, snorm_ref, vgain_ref, gmw_ref, gmb_ref, wout_ref) = ins[n_state:]
    xo_ref, convo_ref, sso_ref = outs[:3]
    gmv_ref = outs[3] if emit_gmv else None
    xpad_ref, xs_ref, cs_ref, y_ref, mix_ref, ht_ref = scratch
    pairs_per_group = SSM_HEADS // SSM_GROUPS // 2
    m = ns * tl
    nck = tl // tc
    nchunks = m // tc
    gm = GM_GROUPS * GM_GDIM

    def init():
        if not has_state:
            for j in range(xpad_ref.shape[1]):
                xpad_ref[:, j, 0:SUBLANES, :] = jnp.zeros((ns, SUBLANES, LANES), F32)
            ht_ref[...] = jnp.zeros(ht_ref.shape, F32)
            return
        _pad_store(xpad_ref, 0, sconv_ref[...])
        for s in range(ns):
            for pr in range(SSM_HEADS // 2):
                g, q = divmod(pr, pairs_per_group)
                ht_ref[s, g, :, q * 2 * SSM_HEAD_DIM:(q + 1) * 2 * SSM_HEAD_DIM] = sssm_ref[s, pr].T

    def final():
        for s in range(ns):
            for pr in range(SSM_HEADS // 2):
                g, q = divmod(pr, pairs_per_group)
                sso_ref[s, pr] = ht_ref[s, g, :, q * 2 * SSM_HEAD_DIM:(q + 1) * 2 * SSM_HEAD_DIM].T

    def norm(x):
        return _rms(x, nm_ref[...]).astype(BF16)

    outputs = [(wuvz_ref, 0, 2 * gm + SSM_DIM, F32), (wxbc_ref, 0, SSM_CONV_DIM, F32), (wdt_ref, 0, DT_LANES, F32)]

    def body(x, projected, tick):
        uvz, xbc, dtr = projected

        ri = lax.broadcasted_iota(jnp.int32, (n_gm, n_gm), 0)
        ci = lax.broadcasted_iota(jnp.int32, (n_gm, n_gm), 1)
        gm_mask = (ci // CHUNK) <= (ri // CHUNK)
        for g in range(GM_GROUPS):
            lanes = slice(g * GM_GDIM, (g + 1) * GM_GDIM)
            u_g = jax.nn.gelu(uvz.cols(g * GM_GDIM, (g + 1) * GM_GDIM))
            v_g = jax.nn.gelu(uvz.cols(gm + g * GM_GDIM, gm + (g + 1) * GM_GDIM))
            v_g = v_g * lax.rsqrt(jnp.mean(v_g * v_g, axis=-1, keepdims=True) + EPS) * vgain_ref[:, lanes]
            if emit_gmv:
                gmv_ref[:, :, lanes] = v_g.reshape(ns, tl, GM_GDIM)
            w_g = jnp.where(gm_mask, gmw_ref[g], 0.0).astype(BF16)
            v_gb = v_g.astype(BF16)
            for c in range(m // n_gm):
                rows = slice(c * n_gm, (c + 1) * n_gm)
                s = _dot(w_g, v_gb[rows]) + gmb_ref[:, lanes]
                mix_ref[rows, lanes] = (u_g[rows] * s).astype(BF16)
            if g % 2 == 1:
                tick()
        x = x + _dot(mix_ref[:, :gm], wout_ref[:gm, :D_MODEL])

        for j in range(SSM_CONV_DIM // LANES):
            xpad_ref[:, j, SUBLANES:, :] = xbc.cols(j * LANES, (j + 1) * LANES).reshape(ns, tl, LANES)
        xc = _silu(_causal_conv(xpad_ref, cw_ref, cb_ref, tl)).reshape(m, SSM_CONV_DIM)
        tail = _pad_load(xpad_ref, tl, SUBLANES)
        _pad_store(xpad_ref, 0, tail)
        convo_ref[...] = tail
        tick()
        xh = xc[:, :SSM_DIM]
        bm = xc[:, SSM_DIM:SSM_DIM + SSM_GROUPS * SSM_STATE].astype(BF16)
        cm = xc[:, SSM_DIM + SSM_GROUPS * SSM_STATE:].astype(BF16)

        dt_n = _softplus(dtr.cols(0, DT_LANES) + dtb_ref[...])
        da_n = dt_n * (-jnp.exp(alog_ref[...]))

        def split3(v):
            hi = v.astype(BF16).astype(F32)
            r1 = v - hi
            mid = r1.astype(BF16).astype(F32)
            return hi, mid, r1 - mid

        lane_grp = lax.broadcasted_iota(jnp.int32, (m, DT_LANES), 1) // SSM_HEADS

        def expand(v):
            hi, mid, lo = split3(v)
            parts = jnp.where(lane_grp == 0, hi, jnp.where(lane_grp == 1, mid, lo))
            return _dot(parts.astype(BF16), expd_ref[...])

        ri = lax.broadcasted_iota(jnp.int32, (m, m), 0)
        ci = lax.broadcasted_iota(jnp.int32, (m, m), 1)
        tri = jnp.where(((ri // tc) == (ci // tc)) & (ci <= ri), 1.0, 0.0).astype(BF16)
        cs3 = _dot(tri, jnp.concatenate(split3(da_n), axis=1).astype(BF16))
        cs_n = cs3[:, :DT_LANES] + cs3[:, DT_LANES:2 * DT_LANES] + cs3[:, 2 * DT_LANES:]
        cs_ref[...] = expand(cs_n)
        dtx = expand(dt_n)
        xs_ref[...] = xh * dtx
        y_ref[...] = dskip_ref[...] * xh
        tick()

        def dup_rows(b_g):
            parts = []
            for c in range(nchunks):
                blk = b_g[c * tc:(c + 1) * tc]
                if tc < CHUNK:
                    blk = jnp.concatenate([blk, jnp.zeros((CHUNK - tc, SSM_STATE), BF16)], axis=0)
                parts += [blk, blk]
            return jnp.concatenate(parts, axis=0)

        cb2 = [_dot_nt(cm[:, g * SSM_STATE:(g + 1) * SSM_STATE], dup_rows(bm[:, g * SSM_STATE:(g + 1) * SSM_STATE]))
               for g in range(SSM_GROUPS)]

        t_i = lax.broadcasted_iota(jnp.int32, (tc, SSM_DIM), 0)
        s_i = lax.broadcasted_iota(jnp.int32, (tc, SSM_DIM), 1) % SSM_HEAD_DIM
        causal = s_i <= t_i
        diag = s_i == t_i
        quad_head = lax.broadcasted_iota(jnp.int32, (tc, 4 * SSM_HEAD_DIM), 1) // SSM_HEAD_DIM
        heads_per_group = SSM_HEADS // SSM_GROUPS

        for c in range(nchunks):
            seq = c // nck
            rows = slice(c * tc, (c + 1) * tc)
            cs_c = cs_ref[rows, :]
            cs_row = jnp.sum(jnp.where(diag, cs_c, 0.0), axis=0, keepdims=True)
            cs_last = cs_c[tc - 1:tc, :]
            lmat = jnp.exp(jnp.where(causal, cs_c - cs_row, -1e30))
            cb_c = jnp.concatenate(
                [cb2[g][rows, c * 2 * CHUNK:(c + 1) * 2 * CHUNK] for g in range(SSM_GROUPS)
                 for _ in range(heads_per_group // 2)], axis=1)
            mm = (lmat * cb_c).astype(BF16)
            xs_c = xs_ref[rows, :]
            y_parts = []
            for q in range(SSM_HEADS // 4):
                ql = slice(q * 4 * SSM_HEAD_DIM, (q + 1) * 4 * SSM_HEAD_DIM)
                xq = xs_c[:, ql]
                blocks = []
                for hh in range(4):
                    blk = jnp.where(quad_head == hh, xq, 0.0).astype(BF16)
                    if tc < CHUNK:
                        blk = jnp.concatenate([blk, jnp.zeros((CHUNK - tc, 4 * SSM_HEAD_DIM), BF16)], axis=0)
                    blocks.append(blk)
                y_parts.append(_dot(mm[:, ql], jnp.concatenate(blocks, axis=0)))
            y_c = jnp.concatenate(y_parts, axis=1)
            ecs = jnp.exp(cs_c)
            xsd = (xs_c * jnp.exp(cs_last - cs_c)).astype(BF16)
            e_last = ecs[tc - 1:tc, :]
            y_off = []
            for g in range(SSM_GROUPS):
                gl = slice(g * GROUP_LANES, (g + 1) * GROUP_LANES)
                sl = slice(g * SSM_STATE, (g + 1) * SSM_STATE)
                h_prev = ht_ref[seq, g]
                y_off.append(_dot(cm[rows, sl], h_prev.astype(BF16)))
                ht_ref[seq, g] = h_prev * e_last[:, gl] + _dot_tn(bm[rows, sl], xsd[:, gl])
            y_ref[rows, :] = y_ref[rows, :] + y_c + jnp.concatenate(y_off, axis=1) * ecs
            tick()

        yb = _rms(y_ref[...] * _silu(uvz.cols(2 * gm, 2 * gm + SSM_DIM)), snorm_ref[...])
        xo_ref[...] = (x + _dot(yb.astype(BF16), wout_ref[gm:, :D_MODEL])).reshape(ns, tl, D_MODEL)

    _skewed(pl.program_id(0), norm, outputs, body, init, x_refs, bufs, nl, final)


def _mixer1_kernel(*refs, ns, tl, nl, skew, has_state):
    n_state = 1 if has_state else 0
    x_refs, ins, outs, scratch, bufs = _split_refs(refs, 2 if skew else 1, n_state + 10, 2, 1)
    sret_ref = ins[0] if has_state else None
    (nm_ref, win_ref, cos_ref, sin_ref, dmat_ref, qdec_ref, kdec_ref, cdec_ref, rnorm_ref,
     wout_ref) = ins[n_state:]
    xo_ref, s_ref = outs
    (mix_ref,) = scratch
    m = ns * tl

    def init():
        s_ref[...] = sret_ref[...] if has_state else jnp.zeros(s_ref.shape, F32)

    def norm(x):
        return _rms(x, nm_ref[...]).astype(BF16)

    outputs = [(win_ref, 0, 2 * RET_QK, F32), (win_ref, 2 * RET_QK, RET_V, BF16), (win_ref, 2 * RET_QK + RET_V, RET_V, F32)]

    def body(x, projected, tick):
        qk, vv, gg = projected
        cos = jnp.concatenate([cos_ref[...]] * ns, axis=0) if ns > 1 else cos_ref[...]
        sin = jnp.concatenate([sin_ref[...]] * ns, axis=0) if ns > 1 else sin_ref[...]
        nseg = m // dmat_ref.shape[1]
        seg = m // nseg
        qdec = jnp.concatenate([qdec_ref[...]] * nseg, axis=0) if nseg > 1 else qdec_ref[...]
        kdec = jnp.concatenate([kdec_ref[...]] * nseg, axis=0) if nseg > 1 else kdec_ref[...]
        scale = RET_DK ** -0.5
        for hd in range(RET_HEADS):
            kl = slice(hd * RET_DK, (hd + 1) * RET_DK)
            vl = slice(hd * RET_DV, (hd + 1) * RET_DV)
            q = qk.cols(hd * RET_DK, (hd + 1) * RET_DK)
            k = qk.cols(RET_QK + hd * RET_DK, RET_QK + (hd + 1) * RET_DK)
            v = vv.cols(hd * RET_DV, (hd + 1) * RET_DV)
            gate = gg.cols(hd * RET_DV, (hd + 1) * RET_DV)
            q = q * cos + pltpu.roll(q, RET_DK // 2, axis=1) * sin
            k = (k * cos + pltpu.roll(k, RET_DK // 2, axis=1) * sin) * scale
            qb, kb = q.astype(BF16), k.astype(BF16)
            y_intra = []
            for sg in range(nseg):
                rows = slice(sg * seg, (sg + 1) * seg)
                sc = _dot_nt(qb[rows], kb[rows]) * dmat_ref[hd]
                y_intra.append(_dot(sc.astype(BF16), v[rows]))
            y = jnp.concatenate(y_intra, axis=0) if nseg > 1 else y_intra[0]
            qd = (q * qdec[:, kl]).astype(BF16)
            kd = (k * kdec[:, kl]).astype(BF16)
            y_inter = []
            for s in range(ns):
                rows = slice(s * tl, (s + 1) * tl)
                s_prev = s_ref[s, hd]
                y_inter.append(_dot(qd[rows], s_prev.astype(BF16)))
                s_ref[s, hd] = s_prev * cdec_ref[:, vl] + _dot_tn(kd[rows], v[rows])
            y = y + (jnp.concatenate(y_inter, axis=0) if ns > 1 else y_inter[0])
            if hd % 2 == 0:
                tick()
            mu = jnp.mean(y, axis=-1, keepdims=True)
            yc = y - mu
            var = jnp.mean(yc * yc, axis=-1, keepdims=True)
            yn = yc * lax.rsqrt(var + EPS) * rnorm_ref[:, vl]
            mix_ref[:, vl] = (_silu(gate) * yn).astype(BF16)
            tick()
            if hd % 2 == 1:
                pair = slice((hd - 1) * RET_DV, (hd + 1) * RET_DV)
                x = x + _dot(mix_ref[:, pair], wout_ref[pair, :D_MODEL])
        xo_ref[...] = x.reshape(ns, tl, D_MODEL)

    _skewed(pl.program_id(0), norm, outputs, body, init, x_refs, bufs, nl)


def _ffn_kernel(*refs, ns, tl, nl, final_norm, skew, has_state):
    n_state = 1 if has_state else 0
    x_refs, ins, outs, scratch, bufs = _split_refs(refs, 2 if skew else 1, n_state + 11, 2, 1)
    p_ref = ins[0]
    sbuf_ref = ins[1] if has_state else None
    (nf_ref, wg_ref, wu_ref, cw_ref, cb_ref, wd_ref, npl_ref, wpg_ref, wpp_ref, nfin_ref) = ins[1 + n_state:]
    xo_ref, bufo_ref = outs
    (gpad_ref,) = scratch
    m = ns * tl

    def init():
        if has_state:
            _pad_store(gpad_ref, 0, sbuf_ref[...])
        else:
            for j in range(gpad_ref.shape[1]):
                gpad_ref[:, j, 0:SUBLANES, :] = jnp.zeros((ns, SUBLANES, LANES), F32)

    def norm(x):
        return _rms(x, nf_ref[...]).astype(BF16)

    outputs = []

    def body(x, projected, tick):
        pp = _dot(p_ref[...].reshape(m, PLE_DIM).astype(BF16), wpp_ref[:, :D_MODEL])
        h = norm(x)
        _pad_store(gpad_ref, SUBLANES, _dot(h, wg_ref[...]).reshape(ns, tl, D_FF))
        gate = _causal_conv(gpad_ref, cw_ref, cb_ref, tl).reshape(m, D_FF)
        tail = _pad_load(gpad_ref, tl, SUBLANES)
        _pad_store(gpad_ref, 0, tail)
        bufo_ref[...] = tail
        act = (_silu(gate) * _dot(h, wu_ref[...])).astype(BF16)
        x = x + _dot(act, wd_ref[:, :D_MODEL])
        pgate = jax.nn.sigmoid(_dot(_rms(x, npl_ref[...]).astype(BF16), wpg_ref[:, :D_MODEL]))
        x = x + pgate * pp
        if final_norm:
            x = _rms(x, nfin_ref[...])
        xo_ref[...] = x.reshape(ns, tl, D_MODEL)

    _skewed(pl.program_id(0), norm, outputs, body, init, x_refs, bufs, nl)


class _Tiling:
    def __init__(self, bsz, length, ns, tl, skew):
        self.ns, self.tl, self.skew = ns, tl, skew
        self.nl = length // tl
        self.n_tiles = (bsz // ns) * self.nl
        self.grid = (self.n_tiles + 1,) if skew else (self.n_tiles,)

    def body_tile(self, t):
        return jnp.maximum(t - 1, 0) if self.skew else t

    def proj_tile(self, t):
        return jnp.minimum(t, self.n_tiles - 1)

    def x_specs(self, width):
        blk = (self.ns, self.tl, width)
        body = pl.BlockSpec(blk, lambda t: (self.body_tile(t) // self.nl, self.body_tile(t) % self.nl, 0))
        if not self.skew:
            return [body]
        return [pl.BlockSpec(blk, lambda t: (self.proj_tile(t) // self.nl, self.proj_tile(t) % self.nl, 0)), body]

    def tile(self, width):
        return self.x_specs(width)[-1]

    def state(self, tail):
        zeros = (0,) * len(tail)
        return pl.BlockSpec((self.ns,) + tail, lambda t: (self.body_tile(t) // self.nl,) + zeros)

    def const(self, shape, block=None):
        if block is not None:
            shape, index = block
            return pl.BlockSpec(shape, lambda t: index, pipeline_mode=pl.Buffered(1))
        zeros = (0,) * len(shape)
        return pl.BlockSpec(shape, lambda t: zeros, pipeline_mode=pl.Buffered(1))

    def layer(self, arr, layer):
        return self.const(None, ((None,) + arr.shape[1:], (layer,) + (0,) * (arr.ndim - 1)))

    def bufs(self, shapes):
        return [pltpu.VMEM(s, d) for _ in range(2) for s, d in shapes] if self.skew else []


def _params():
    return pltpu.CompilerParams(dimension_semantics=("arbitrary",), vmem_limit_bytes=VMEM_LIMIT)


def _mixer0(x, sconv, sssm, w, *, ns, tl, emit_gmv, skew):
    bsz, length, _ = x.shape
    tg = _Tiling(bsz, length, ns, tl, skew)
    tc = min(tl, CHUNK)
    n_gm = min(tl, GM_CHUNK)
    m = ns * tl
    has_state = sconv is not None
    pair_shape = (SSM_HEADS // 2, 2 * SSM_HEAD_DIM, SSM_STATE)
    states = [_header(sconv), sssm.reshape((bsz,) + pair_shape)] if has_state else []
    state_specs = [tg.state((SUBLANES, SSM_CONV_DIM)), tg.state(pair_shape)]
    consts = [w['nm0'], w['wuvz'], w['wxbc'], w['wdt'], w['dtb'], w['alog'], w['expd'], w['dskip'], w['ssm_cw'],
              w['ssm_cb'], w['snorm'], w['vgain'], w['gmw'][:, :n_gm, :n_gm], w['gmb'][:n_gm], w['wout0']]
    out_shape = [jax.ShapeDtypeStruct(x.shape, F32),
                 jax.ShapeDtypeStruct((bsz, SUBLANES, SSM_CONV_DIM), F32),
                 jax.ShapeDtypeStruct((bsz,) + pair_shape, F32)]
    out_specs = [tg.tile(D_MODEL)] + state_specs
    if emit_gmv:
        out_shape.append(jax.ShapeDtypeStruct((bsz, length, GM_GROUPS * GM_GDIM), F32))
        out_specs.append(tg.tile(GM_GROUPS * GM_GDIM))
    outs = pl.pallas_call(
        functools.partial(_mixer0_kernel, ns=ns, tl=tl, nl=tg.nl, tc=tc, n_gm=n_gm, emit_gmv=emit_gmv, skew=skew,
                          has_state=has_state),
        grid=tg.grid,
        in_specs=tg.x_specs(D_MODEL) + state_specs[:len(states)] + [tg.const(c.shape) for c in consts],
        out_specs=out_specs,
        out_shape=out_shape,
        scratch_shapes=[pltpu.VMEM((ns, SSM_CONV_DIM // LANES, SUBLANES + tl, LANES), F32),
                        pltpu.VMEM((m, SSM_DIM), F32), pltpu.VMEM((m, SSM_DIM), F32),
                        pltpu.VMEM((m, SSM_DIM), F32), pltpu.VMEM((m, GM_GROUPS * GM_GDIM), BF16),
                        pltpu.VMEM((ns, SSM_GROUPS, SSM_STATE, GROUP_LANES), F32)]
        + tg.bufs([((m, 2 * GM_GROUPS * GM_GDIM + SSM_DIM), F32), ((m, SSM_CONV_DIM), F32), ((m, DT_LANES), F32)]),
        compiler_params=_params(),
        name="mixer0",
    )(*([x] * len(tg.x_specs(D_MODEL))), *states, *consts)
    conv_o = outs[1][:, SUBLANES - (SSM_CONV - 1):]
    ssm_o = outs[2].reshape(bsz, SSM_HEADS, SSM_HEAD_DIM, SSM_STATE)
    return outs[0], conv_o, ssm_o, (outs[3] if emit_gmv else None)


def _mixer1(x, sret, w, pos0, *, ns, tl, skew):
    bsz, length, _ = x.shape
    tg = _Tiling(bsz, length, ns, tl, skew)
    m = ns * tl
    tabs = _retention_tables(ns if tl < JOINT_ROWS else 1, tl, pos0, length)
    has_state = sret is not None
    state_shape = (RET_HEADS, RET_DK, RET_DV)
    states = [sret] if has_state else []
    consts_a = [w['nm1'], w['win1']]
    consts_b = [tabs['dmat'], tabs['qdec'], tabs['kdec'], tabs['cdec'], w['rnorm'], w['wout1']]
    pos_spec = pl.BlockSpec((tl, RET_DK), lambda t: (tg.body_tile(t) % tg.nl, 0))
    return pl.pallas_call(
        functools.partial(_mixer1_kernel, ns=ns, tl=tl, nl=tg.nl, skew=skew, has_state=has_state),
        grid=tg.grid,
        in_specs=tg.x_specs(D_MODEL) + [tg.state(state_shape)] * len(states)
        + [tg.const(c.shape) for c in consts_a] + [pos_spec, pos_spec]
        + [tg.const(c.shape) for c in consts_b],
        out_specs=[tg.tile(D_MODEL), tg.state(state_shape)],
        out_shape=[jax.ShapeDtypeStruct(x.shape, F32), jax.ShapeDtypeStruct((bsz,) + state_shape, F32)],
        scratch_shapes=[pltpu.VMEM((m, RET_V), BF16)]
        + tg.bufs([((m, 2 * RET_QK), F32), ((m, RET_V), BF16), ((m, RET_V), F32)]),
        compiler_params=_params(),
        name="mixer1",
    )(*([x] * len(tg.x_specs(D_MODEL))), *states, *consts_a, tabs['cos'], tabs['sin'], *consts_b)


def _ffn(x, p, sbuf, w, layer, *, ns, tl, final_norm):
    bsz, length, _ = x.shape
    tg = _Tiling(bsz, length, ns, tl, False)
    has_state = sbuf is not None
    states = [_header(sbuf)] if has_state else []
    stacked = [w['nf'], w['wg'], w['wu'], w['ffn_cw'], w['ffn_cb'], w['wd'], w['npl'], w['wpg'], w['wpp']]
    p_spec = pl.BlockSpec((None, ns, tl, PLE_DIM),
                          lambda t: (layer, tg.body_tile(t) // tg.nl, tg.body_tile(t) % tg.nl, 0))
    x, buf_o = pl.pallas_call(
        functools.partial(_ffn_kernel, ns=ns, tl=tl, nl=tg.nl, final_norm=final_norm, skew=False,
                          has_state=has_state),
        grid=tg.grid,
        in_specs=tg.x_specs(D_MODEL) + [p_spec] + [tg.state((SUBLANES, D_FF))] * len(states)
        + [tg.layer(c, layer) for c in stacked] + [tg.const(w['nfin'].shape)],
        out_specs=[tg.tile(D_MODEL), tg.state((SUBLANES, D_FF))],
        out_shape=[jax.ShapeDtypeStruct(x.shape, F32), jax.ShapeDtypeStruct((bsz, SUBLANES, D_FF), F32)],
        scratch_shapes=[pltpu.VMEM((ns, D_FF // LANES, SUBLANES + tl, LANES), F32)],
        compiler_params=_params(),
        name="ffn%d" % layer,
    )(x, p, *states, *stacked, w['nfin'])
    return x, buf_o[:, SUBLANES - (FFN_CONV - 1):]


def _retention_tables(ns, tl, pos0, length):
    half = RET_DK // 2
    inv = ROPE_BASE ** (-np.arange(half, dtype=np.float64) / half)
    ang = (pos0 + np.arange(length, dtype=np.float64))[:, None] * inv[None]
    cos, sin = np.cos(ang), np.sin(ang)
    lg = np.log(1.0 - 2.0 ** (-5.0 - np.arange(RET_HEADS, dtype=np.float64)))
    m = ns * tl
    t = np.arange(m) % tl
    same_seq = (np.arange(m)[:, None] // tl) == (np.arange(m)[None, :] // tl)
    diff = (t[:, None] - t[None, :]).astype(np.float64)
    dmat = np.where((diff[None] >= 0) & same_seq[None], np.exp(lg[:, None, None] * np.maximum(diff, 0.0)[None]), 0.0)
    qdec = np.exp(lg[None, :] * (t[:, None] + 1.0))
    kdec = np.exp(lg[None, :] * (tl - 1.0 - t)[:, None])
    cdec = np.exp(lg * tl)
    tabs = {
        'cos': np.concatenate([cos, cos], axis=-1),
        'sin': np.concatenate([-sin, sin], axis=-1),
        'dmat': dmat,
        'qdec': np.repeat(qdec, RET_DK, axis=1),
        'kdec': np.repeat(kdec, RET_DK, axis=1),
        'cdec': np.repeat(cdec, RET_DV)[None],
    }
    return {k: jnp.asarray(v, F32) for k, v in tabs.items()}


def _header(buf):
    return jnp.pad(buf, ((0, 0), (SUBLANES - buf.shape[1], 0), (0, 0)))


def _trunk(x, p, sconv, sssm, sret, sffn, pos0, w, *, ns, ns_ret, tl, tl_ffn, emit_gmv, skew):
    x, conv_o, ssm_o, gmv = _mixer0(x, sconv, sssm, w, ns=ns, tl=tl, emit_gmv=emit_gmv, skew=skew)
    x, ffn0 = _ffn(x, p, None if sffn is None else sffn[0], w, 0, ns=ns, tl=tl_ffn, final_norm=False)
    x, ret_o = _mixer1(x, sret, w, pos0, ns=ns_ret, tl=tl, skew=skew and ns_ret == 1)
    x, ffn1 = _ffn(x, p, None if sffn is None else sffn[1], w, 1, ns=ns, tl=tl_ffn, final_norm=True)
    return x, gmv, conv_o, ssm_o, ret_o, jnp.stack([ffn0, ffn1])


def _prepare_weights(norm_mix, norm_ffn, norm_ple, norm_final, l0_w_in, gm_v_gain, gm_w_s, gm_b_s,
                     ssm_conv_w, ssm_conv_b, ssm_dt_bias, ssm_a_log, ssm_d, ssm_norm, l0_w_out,
                     l1_w_in, ret_norm, l1_w_out, ffn_w_gate, ffn_w_up, ffn_conv_w, ffn_conv_b,
                     ffn_w_down, ple_w_gate, ple_w_proj):
    gm = GM_GROUPS * GM_GDIM
    rep = lambda v: jnp.repeat(v, SSM_HEAD_DIM)[None].astype(F32)
    tile = lambda v: jnp.tile(v, DT_LANES // SSM_HEADS)[None].astype(F32)
    lane = jnp.arange(DT_LANES)

    def mxu_weight(wt):
        wt = wt.astype(BF16)
        if (wt.shape[-1] // LANES) % SUBLANES == 0:
            wt = jnp.pad(wt, [(0, 0)] * (wt.ndim - 1) + [(0, LANES)])
        return wt
    return {
        'nm0': norm_mix[0][None], 'nm1': norm_mix[1][None],
        'wuvz': mxu_weight(l0_w_in[:, :2 * gm + SSM_DIM]),
        'wxbc': mxu_weight(l0_w_in[:, 2 * gm + SSM_DIM:2 * gm + SSM_DIM + SSM_CONV_DIM]),
        'wdt': jnp.tile(l0_w_in[:, 2 * gm + SSM_DIM + SSM_CONV_DIM:], (1, DT_LANES // SSM_HEADS)).astype(BF16),
        'dtb': tile(ssm_dt_bias), 'alog': tile(ssm_a_log), 'dskip': rep(ssm_d),
        'expd': ((lane[:, None] < 3 * SSM_HEADS)
                 & (lane[:, None] % SSM_HEADS == jnp.arange(SSM_DIM)[None] // SSM_HEAD_DIM)).astype(BF16),
        'ssm_cw': ssm_conv_w, 'ssm_cb': ssm_conv_b[None], 'snorm': ssm_norm[None], 'vgain': gm_v_gain[None],
        'gmw': gm_w_s, 'gmb': jnp.repeat(gm_b_s.T, GM_GDIM, axis=1),
        'wout0': mxu_weight(l0_w_out),
        'win1': mxu_weight(l1_w_in), 'rnorm': ret_norm[None], 'wout1': mxu_weight(l1_w_out),
        'nf': norm_ffn[:, None], 'wg': ffn_w_gate.astype(BF16), 'wu': ffn_w_up.astype(BF16),
        'ffn_cw': ffn_conv_w, 'ffn_cb': ffn_conv_b[:, None], 'wd': mxu_weight(ffn_w_down),
        'npl': norm_ple[:, None], 'wpg': mxu_weight(ple_w_gate), 'wpp': mxu_weight(ple_w_proj),
        'nfin': norm_final[None],
    }


def kernel(x_prompt, x_sample, state_ssm_conv, state_ssm, state_ret, state_ffn_conv, p_prompt, p_sample, norm_mix, norm_ffn, norm_ple, norm_final, l0_w_in, gm_v_gain, gm_w_s, gm_b_s, ssm_conv_w, ssm_conv_b, ssm_dt_bias, ssm_a_log, ssm_d, ssm_norm, l0_w_out, l1_w_in, ret_norm, l1_w_out, ffn_w_gate, ffn_w_up, ffn_conv_w, ffn_conv_b, ffn_w_down, ple_w_gate, ple_w_proj):
    w = _prepare_weights(norm_mix, norm_ffn, norm_ple, norm_final, l0_w_in, gm_v_gain, gm_w_s, gm_b_s,
                         ssm_conv_w, ssm_conv_b, ssm_dt_bias, ssm_a_log, ssm_d, ssm_norm, l0_w_out,
                         l1_w_in, ret_norm, l1_w_out, ffn_w_gate, ffn_w_up, ffn_conv_w, ffn_conv_b,
                         ffn_w_down, ple_w_gate, ple_w_proj)
    y_p, _, conv_p, ssm_p, ret_p, ffn_p = _trunk(
        x_prompt, p_prompt, None, None, None, None, 0, w, ns=1, ns_ret=1, tl=256, tl_ffn=512, emit_gmv=False,
        skew=True)
    y_s, gmv_s, conv_s, ssm_s, ret_s, ffn_s = _trunk(
        x_sample, p_sample, state_ssm_conv, state_ssm, state_ret, state_ffn_conv, PAST_LEN, w,
        ns=8, ns_ret=4, tl=32, tl_ffn=32, emit_gmv=True, skew=False)
    return (y_p, y_s, conv_p, conv_s, ssm_p, ssm_s, ret_p, ret_s, ffn_p, ffn_s, gmv_s)
```

```python
import functools

import jax
import jax.numpy as jnp
import numpy as np
from jax import lax
from jax.experimental import pallas as pl
from jax.experimental.pallas import tpu as pltpu

F32 = jnp.float32
BF16 = jnp.bfloat16

D_MODEL = 1024
DEPTH = 2
PAST_LEN = 1024
CHUNK = 64
PLE_DIM = 256
EPS = 1e-6
GM_CHUNK = 128
GM_GROUPS = 8
GM_GDIM = 128
SSM_HEADS = 16
SSM_HEAD_DIM = 64
SSM_DIM = 1024
SSM_GROUPS = 2
SSM_STATE = 128
SSM_CONV = 4
SSM_CONV_DIM = 1536
RET_HEADS = 8
RET_DK = 128
RET_DV = 256
RET_QK = 1024
RET_V = 2048
ROPE_BASE = 10000.0
D_FF = 2816
FFN_CONV = 3

SUBLANES = 8
LANES = 128
GROUP_LANES = SSM_HEADS // SSM_GROUPS * SSM_HEAD_DIM
DT_LANES = 128
VMEM_LIMIT = 62 * 1024 * 1024


def _dot(a, b):
    return jnp.dot(a, b, preferred_element_type=F32)


def _dot_nt(a, b):
    return lax.dot_general(a, b, (((1,), (1,)), ((), ())), preferred_element_type=F32)


def _dot_tn(a, b):
    return lax.dot_general(a, b, (((0,), (0,)), ((), ())), preferred_element_type=F32)


def _rms(x, g):
    return x * lax.rsqrt(jnp.mean(x * x, axis=-1, keepdims=True) + EPS) * g


def _silu(x):
    return x * jax.nn.sigmoid(x)


def _softplus(x):
    return jnp.maximum(x, 0.0) + jnp.log1p(jnp.exp(-jnp.abs(x)))


def _pad_store(pad_ref, start, value):
    rows = value.shape[1]
    for j in range(pad_ref.shape[1]):
        pad_ref[:, j, start:start + rows, :] = value[:, :, j * LANES:(j + 1) * LANES]


def _pad_load(pad_ref, start, rows, j0=0, j1=None):
    j1 = pad_ref.shape[1] if j1 is None else j1
    return jnp.concatenate([pad_ref[:, j, start:start + rows, :] for j in range(j0, j1)], axis=-1)


def _causal_conv(pad_ref, w_ref, b_ref, tl, j0=0, j1=None):
    j1 = pad_ref.shape[1] if j1 is None else j1
    width = w_ref.shape[0]
    lanes = slice(j0 * LANES, j1 * LANES)
    y = b_ref[:, lanes][None]
    for k in range(width):
        y = y + w_ref[k:k + 1, lanes][None] * _pad_load(pad_ref, SUBLANES - (width - 1) + k, tl, j0, j1)
    return y


class _Cols:
    def __init__(self, src):
        self.src = src

    def cols(self, a, b):
        return self.src[:, a:b]


PROJ_BLOCK = 512
LATE_BLOCK = 512
JOINT_ROWS = 128
BAND_ROWS = 256


def _proj_spec(outputs):
    spec = []
    for o, (w_ref, first, n_cols, _) in enumerate(outputs):
        for c in range(0, n_cols, PROJ_BLOCK):
            spec.append((o, w_ref, first + c, c, min(PROJ_BLOCK, n_cols - c)))
    return spec


def _skewed(t, norm_fn, outputs, body_fn, init_fn, x_refs, bufs, nl, final_fn=None):
    spec = _proj_spec(outputs)
    if bufs is None:
        (x_ref,) = x_refs

        @pl.when(t % nl == 0)
        def _():
            init_fn()

        x = x_ref[...].reshape(-1, D_MODEL)
        h = norm_fn(x)
        parts = [[] for _ in outputs]
        for o, w_ref, wc, _, n in spec:
            parts[o].append(_dot(h, w_ref[:, wc:wc + n]).astype(outputs[o][3]))
        body_fn(x, tuple(_Cols(jnp.concatenate(p, axis=1) if len(p) > 1 else p[0]) for p in parts), lambda n=1: None,
                h)
        if final_fn is not None:
            @pl.when(t % nl == nl - 1)
            def _():
                final_fn()
        return

    xa_ref, xb_ref = x_refs

    @pl.when(t == 0)
    def _():
        for r in bufs[1]:
            r[...] = jnp.zeros(r.shape, r.dtype)

    @pl.when(jnp.maximum(t - 1, 0) % nl == 0)
    def _():
        init_fn()

    for parity in range(2):
        @pl.when(t % 2 == parity)
        def _(parity=parity):
            h = norm_fn(xa_ref[...].reshape(-1, D_MODEL))
            pieces = iter(spec)

            def tick(n=1):
                for _ in range(n):
                    piece = next(pieces, None)
                    if piece is not None:
                        o, w_ref, wc, oc, width = piece
                        bufs[parity][o][:, oc:oc + width] = _dot(h, w_ref[:, wc:wc + width]).astype(outputs[o][3])

            body_fn(xb_ref[...].reshape(-1, D_MODEL), tuple(_Cols(r) for r in bufs[1 - parity]), tick, None)
            tick(len(spec))

    if final_fn is not None:
        @pl.when(jnp.maximum(t - 1, 0) % nl == nl - 1)
        def _():
            final_fn()


def _split_refs(refs, n_x, n_in, n_out, n_scratch):
    refs = list(refs)
    x_refs, refs = refs[:n_x], refs[n_x:]
    ins, refs = refs[:n_in], refs[n_in:]
    outs, refs = refs[:n_out], refs[n_out:]
    scratch, refs = refs[:n_scratch], refs[n_scratch:]
    bufs = (refs[:len(refs) // 2], refs[len(refs) // 2:]) if refs else None
    return x_refs, ins, outs, scratch, bufs


def _mixer0_kernel(*refs, ns, tl, nl, tc, n_gm, emit_gmv, skew, has_state):
    n_state = 2 if has_state else 0
    x_refs, ins, outs, scratch, bufs = _split_refs(refs, 2 if skew else 1, n_state + 15, 4 if emit_gmv else 3, 6)
    sconv_ref, sssm_ref = ins[:n_state] if has_state else (None, None)
    (nm_ref, wuvz_ref, wxbc_ref, wdt_ref, dtb_ref, alog_ref, expd_ref, dskip_ref, cw_ref,
     cb_ref, snorm_ref, vgain_ref, gmw_ref, gmb_ref, wout_ref) = ins[n_state:]
    xo_ref, convo_ref, sso_ref = outs[:3]
    gmv_ref = outs[3] if emit_gmv else None
    xpad_ref, xs_ref, cs_ref, y_ref, mix_ref, ht_ref = scratch
    pairs_per_group = SSM_HEADS // SSM_GROUPS // 2
    m = ns * tl
    nck = tl // tc
    nchunks = m // tc
    gm = GM_GROUPS * GM_GDIM

    def init():
        if not has_state:
            for j in range(xpad_ref.shape[1]):
                xpad_ref[:, j, 0:SUBLANES, :] = jnp.zeros((ns, SUBLANES, LANES), F32)
            ht_ref[...] = jnp.zeros(ht_ref.shape, F32)
            return
        _pad_store(xpad_ref, 0, sconv_ref[...])
        for s in range(ns):
            for pr in range(SSM_HEADS // 2):
                g, q = divmod(pr, pairs_per_group)
                ht_ref[s, g, :, q * 2 * SSM_HEAD_DIM:(q + 1) * 2 * SSM_HEAD_DIM] = sssm_ref[s, pr].T

    def final():
        for s in range(ns):
            for pr in range(SSM_HEADS // 2):
                g, q = divmod(pr, pairs_per_group)
                sso_ref[s, pr] = ht_ref[s, g, :, q * 2 * SSM_HEAD_DIM:(q + 1) * 2 * SSM_HEAD_DIM].T

    def norm(x):
        return _rms(x, nm_ref[...]).astype(BF16)

    assert not skew
    outputs = [(wuvz_ref, 0, 2 * gm, F32), (wxbc_ref, 0, SSM_CONV_DIM, F32), (wdt_ref, 0, DT_LANES, F32)]

    def body(x, projected, tick, h):
        uvz, xbc, dtr = projected

        z_parts, xg_parts = {}, {}

        def z_block(c0):
            z_parts[c0] = _dot(h, wuvz_ref[:, 2 * gm + c0:2 * gm + c0 + LATE_BLOCK])

        def out_gm_block(c0):
            xg_parts[c0] = x[:, c0:c0 + LATE_BLOCK] + _dot(mix_ref[:, :gm], wout_ref[:gm, c0:c0 + LATE_BLOCK])

        late = [functools.partial(f, c0) for c0 in range(0, SSM_DIM, LATE_BLOCK) for f in (out_gm_block, z_block)]

        def emit_late():
            if late:
                late.pop(0)()

        ri = lax.broadcasted_iota(jnp.int32, (n_gm, n_gm), 0)
        ci = lax.broadcasted_iota(jnp.int32, (n_gm, n_gm), 1)
        gm_mask = (ci // CHUNK) <= (ri // CHUNK)
        for g in range(GM_GROUPS):
            lanes = slice(g * GM_GDIM, (g + 1) * GM_GDIM)
            u_g = jax.nn.gelu(uvz.cols(g * GM_GDIM, (g + 1) * GM_GDIM))
            v_g = jax.nn.gelu(uvz.cols(gm + g * GM_GDIM, gm + (g + 1) * GM_GDIM))
            v_g = v_g * lax.rsqrt(jnp.mean(v_g * v_g, axis=-1, keepdims=True) + EPS) * vgain_ref[:, lanes]
            if emit_gmv:
                gmv_ref[:, :, lanes] = v_g.reshape(ns, tl, GM_GDIM)
            w_g = jnp.where(gm_mask, gmw_ref[g], 0.0).astype(BF16)
            v_gb = v_g.astype(BF16)
            for c in range(m // n_gm):
                rows = slice(c * n_gm, (c + 1) * n_gm)
                s = _dot(w_g, v_gb[rows]) + gmb_ref[:, lanes]
                mix_ref[rows, lanes] = (u_g[rows] * s).astype(BF16)

        for j in range(SSM_CONV_DIM // LANES):
            xpad_ref[:, j, SUBLANES:, :] = xbc.cols(j * LANES, (j + 1) * LANES).reshape(ns, tl, LANES)
        emit_late()
        xc = _silu(_causal_conv(xpad_ref, cw_ref, cb_ref, tl)).reshape(m, SSM_CONV_DIM)
        tail = _pad_load(xpad_ref, tl, SUBLANES)
        _pad_store(xpad_ref, 0, tail)
        convo_ref[...] = tail
        emit_late()
        xh = xc[:, :SSM_DIM]
        bm = xc[:, SSM_DIM:SSM_DIM + SSM_GROUPS * SSM_STATE].astype(BF16)
        cm = xc[:, SSM_DIM + SSM_GROUPS * SSM_STATE:].astype(BF16)

        dt_n = _softplus(dtr.cols(0, DT_LANES) + dtb_ref[...])
        da_n = dt_n * (-jnp.exp(alog_ref[...]))
        emit_late()

        def split3(v):
            hi = v.astype(BF16).astype(F32)
            r1 = v - hi
            mid = r1.astype(BF16).astype(F32)
            return hi, mid, r1 - mid

        lane_grp = lax.broadcasted_iota(jnp.int32, (m, DT_LANES), 1) // SSM_HEADS

        def expand(v):
            hi, mid, lo = split3(v)
            parts = jnp.where(lane_grp == 0, hi, jnp.where(lane_grp == 1, mid, lo))
            return _dot(parts.astype(BF16), expd_ref[:, :SSM_DIM])

        band = min(m, BAND_ROWS)
        ri = lax.broadcasted_iota(jnp.int32, (band, band), 0)
        ci = lax.broadcasted_iota(jnp.int32, (band, band), 1)
        tri = jnp.where(((ri // tc) == (ci // tc)) & (ci <= ri), 1.0, 0.0).astype(BF16)
        da3 = jnp.concatenate(split3(da_n), axis=1).astype(BF16)
        cs3 = [_dot(tri, da3[b * band:(b + 1) * band]) for b in range(m // band)]
        cs3 = jnp.concatenate(cs3, axis=0) if len(cs3) > 1 else cs3[0]
        cs_n = cs3[:, :DT_LANES] + cs3[:, DT_LANES:2 * DT_LANES] + cs3[:, 2 * DT_LANES:]
        cs_ref[...] = expand(cs_n)
        dtx = expand(dt_n)
        xs_ref[...] = xh * dtx
        y_ref[...] = dskip_ref[...] * xh
        emit_late()

        cpb = band // tc

        def dup_rows(b_g):
            parts = []
            for k in range(cpb):
                blk = b_g[k * tc:(k + 1) * tc]
                if tc < CHUNK:
                    blk = jnp.concatenate([blk, jnp.zeros((CHUNK - tc, SSM_STATE), BF16)], axis=0)
                parts += [blk, blk]
            return jnp.concatenate(parts, axis=0)

        cb2 = [[_dot_nt(cm[b * band:(b + 1) * band, g * SSM_STATE:(g + 1) * SSM_STATE],
                        dup_rows(bm[b * band:(b + 1) * band, g * SSM_STATE:(g + 1) * SSM_STATE]))
                for g in range(SSM_GROUPS)] for b in range(m // band)]

        t_i = lax.broadcasted_iota(jnp.int32, (tc, SSM_DIM), 0)
        s_i = lax.broadcasted_iota(jnp.int32, (tc, SSM_DIM), 1) % SSM_HEAD_DIM
        causal = s_i <= t_i
        diag = s_i == t_i
        quad_head = lax.broadcasted_iota(jnp.int32, (tc, 4 * SSM_HEAD_DIM), 1) // SSM_HEAD_DIM
        heads_per_group = SSM_HEADS // SSM_GROUPS

        for c in range(nchunks):
            seq = c // nck
            rows = slice(c * tc, (c + 1) * tc)
            cs_c = cs_ref[rows, :]
            cs_row = jnp.sum(jnp.where(diag, cs_c, 0.0), axis=0, keepdims=True)
            cs_last = cs_c[tc - 1:tc, :]
            lmat = jnp.exp(jnp.where(causal, cs_c - cs_row, -1e30))
            b, k = divmod(c, cpb)
            cb_c = jnp.concatenate(
                [cb2[b][g][k * tc:(k + 1) * tc, k * 2 * CHUNK:(k + 1) * 2 * CHUNK] for g in range(SSM_GROUPS)
                 for _ in range(heads_per_group // 2)], axis=1)
            mm = (lmat * cb_c).astype(BF16)
            xs_c = xs_ref[rows, :]
            y_parts = []
            for q in range(SSM_HEADS // 4):
                ql = slice(q * 4 * SSM_HEAD_DIM, (q + 1) * 4 * SSM_HEAD_DIM)
                xq = xs_c[:, ql]
                blocks = []
                for hh in range(4):
                    blk = jnp.where(quad_head == hh, xq, 0.0).astype(BF16)
                    if tc < CHUNK:
                        blk = jnp.concatenate([blk, jnp.zeros((CHUNK - tc, 4 * SSM_HEAD_DIM), BF16)], axis=0)
                    blocks.append(blk)
                y_parts.append(_dot(mm[:, ql], jnp.concatenate(blocks, axis=0)))
            y_c = jnp.concatenate(y_parts, axis=1)
            ecs = jnp.exp(cs_c)
            xsd = (xs_c * jnp.exp(cs_last - cs_c)).astype(BF16)
            e_last = ecs[tc - 1:tc, :]
            y_off = []
            for g in range(SSM_GROUPS):
                gl = slice(g * GROUP_LANES, (g + 1) * GROUP_LANES)
                sl = slice(g * SSM_STATE, (g + 1) * SSM_STATE)
                h_prev = ht_ref[seq, g]
                y_off.append(_dot(cm[rows, sl], h_prev.astype(BF16)))
                ht_ref[seq, g] = h_prev * e_last[:, gl] + _dot_tn(bm[rows, sl], xsd[:, gl])
            y_ref[rows, :] = y_ref[rows, :] + y_c + jnp.concatenate(y_off, axis=1) * ecs
        while late:
            emit_late()

        z = jnp.concatenate([z_parts[c0] for c0 in sorted(z_parts)], axis=1)
        x_gm = jnp.concatenate([xg_parts[c0] for c0 in sorted(xg_parts)], axis=1)
        yb = _rms(y_ref[...] * _silu(z), snorm_ref[...])
        xo_ref[...] = (x_gm + _dot(yb.astype(BF16), wout_ref[gm:, :D_MODEL])).reshape(ns, tl, D_MODEL)

    _skewed(pl.program_id(0), norm, outputs, body, init, x_refs, bufs, nl, final)


def _mixer1_kernel(*refs, ns, tl, nl, skew, has_state):
    n_state = 1 if has_state else 0
    x_refs, ins, outs, scratch, bufs = _split_refs(refs, 2 if skew else 1, n_state + 10, 2, 1)
    sret_ref = ins[0] if has_state else None
    (nm_ref, win_ref, cos_ref, sin_ref, dmat_ref, qdec_ref, kdec_ref, cdec_ref, rnorm_ref,
     wout_ref) = ins[n_state:]
    xo_ref, s_ref = outs
    (mix_ref,) = scratch
    m = ns * tl

    def init():
        s_ref[...] = sret_ref[...] if has_state else jnp.zeros(s_ref.shape, F32)

    def norm(x):
        return _rms(x, nm_ref[...]).astype(BF16)

    outputs = [(win_ref, 0, 2 * RET_QK, F32), (win_ref, 2 * RET_QK, RET_V, BF16), (win_ref, 2 * RET_QK + RET_V, RET_V, F32)]
    n_ticks = RET_HEADS + RET_HEADS // 2
    per_tick = -(-len(_proj_spec(outputs)) // n_ticks)

    def body(x, projected, tick, _h):
        qk, vv, gg = projected
        cos = jnp.concatenate([cos_ref[...]] * ns, axis=0) if ns > 1 else cos_ref[...]
        sin = jnp.concatenate([sin_ref[...]] * ns, axis=0) if ns > 1 else sin_ref[...]
        nseg = m // dmat_ref.shape[1]
        seg = m // nseg
        qdec = jnp.concatenate([qdec_ref[...]] * nseg, axis=0) if nseg > 1 else qdec_ref[...]
        kdec = jnp.concatenate([kdec_ref[...]] * nseg, axis=0) if nseg > 1 else kdec_ref[...]
        scale = RET_DK ** -0.5
        for hd in range(RET_HEADS):
            kl = slice(hd * RET_DK, (hd + 1) * RET_DK)
            vl = slice(hd * RET_DV, (hd + 1) * RET_DV)
            q = qk.cols(hd * RET_DK, (hd + 1) * RET_DK)
            k = qk.cols(RET_QK + hd * RET_DK, RET_QK + (hd + 1) * RET_DK)
            v = vv.cols(hd * RET_DV, (hd + 1) * RET_DV)
            gate = gg.cols(hd * RET_DV, (hd + 1) * RET_DV)
            q = q * cos + pltpu.roll(q, RET_DK // 2, axis=1) * sin
            k = (k * cos + pltpu.roll(k, RET_DK // 2, axis=1) * sin) * scale
            qb, kb = q.astype(BF16), k.astype(BF16)
            y_intra = []
            for sg in range(nseg):
                rows = slice(sg * seg, (sg + 1) * seg)
                sc = _dot_nt(qb[rows], kb[rows]) * dmat_ref[hd]
                y_intra.append(_dot(sc.astype(BF16), v[rows]))
            y = jnp.concatenate(y_intra, axis=0) if nseg > 1 else y_intra[0]
            qd = (q * qdec[:, kl]).astype(BF16)
            kd = (k * kdec[:, kl]).astype(BF16)
            y_inter = []
            for s in range(ns):
                rows = slice(s * tl, (s + 1) * tl)
                s_prev = s_ref[s, hd]
                y_inter.append(_dot(qd[rows], s_prev.astype(BF16)))
                s_ref[s, hd] = s_prev * cdec_ref[:, vl] + _dot_tn(kd[rows], v[rows])
            y = y + (jnp.concatenate(y_inter, axis=0) if ns > 1 else y_inter[0])
            if hd % 2 == 0:
                tick(per_tick)
            mu = jnp.mean(y, axis=-1, keepdims=True)
            yc = y - mu
            var = jnp.mean(yc * yc, axis=-1, keepdims=True)
            yn = yc * lax.rsqrt(var + EPS) * rnorm_ref[:, vl]
            mix_ref[:, vl] = (_silu(gate) * yn).astype(BF16)
            tick(per_tick)
            if hd % 2 == 1:
                pair = slice((hd - 1) * RET_DV, (hd + 1) * RET_DV)
                x = x + _dot(mix_ref[:, pair], wout_ref[pair, :D_MODEL])
        xo_ref[...] = x.reshape(ns, tl, D_MODEL)

    _skewed(pl.program_id(0), norm, outputs, body, init, x_refs, bufs, nl)


def _ffn_kernel(*refs, ns, tl, nl, final_norm, skew, has_state):
    n_state = 1 if has_state else 0
    x_refs, ins, outs, scratch, bufs = _split_refs(refs, 2 if skew else 1, n_state + 11, 2, 1)
    p_ref = ins[0]
    sbuf_ref = ins[1] if has_state else None
    (nf_ref, wg_ref, wu_ref, cw_ref, cb_ref, wd_ref, npl_ref, wpg_ref, wpp_ref, nfin_ref) = ins[1 + n_state:]
    xo_ref, bufo_ref = outs
    (gpad_ref,) = scratch
    m = ns * tl

    def init():
        if has_state:
            _pad_store(gpad_ref, 0, sbuf_ref[...])
        else:
            for j in range(gpad_ref.shape[1]):
                gpad_ref[:, j, 0:SUBLANES, :] = jnp.zeros((ns, SUBLANES, LANES), F32)

    def norm(x):
        return _rms(x, nf_ref[...]).astype(BF16)

    outputs = []

    def body(x, projected, tick, h):
        pp = _dot(p_ref[...].reshape(m, PLE_DIM).astype(BF16), wpp_ref[:, :D_MODEL])
        _pad_store(gpad_ref, SUBLANES, _dot(h, wg_ref[...]).reshape(ns, tl, D_FF))
        gate = _causal_conv(gpad_ref, cw_ref, cb_ref, tl).reshape(m, D_FF)
        tail = _pad_load(gpad_ref, tl, SUBLANES)
        _pad_store(gpad_ref, 0, tail)
        bufo_ref[...] = tail
        act = (_silu(gate) * _dot(h, wu_ref[...])).astype(BF16)
        x = x + _dot(act, wd_ref[:, :D_MODEL])
        pgate = jax.nn.sigmoid(_dot(_rms(x, npl_ref[...]).astype(BF16), wpg_ref[:, :D_MODEL]))
        x = x + pgate * pp
        if final_norm:
            x = _rms(x, nfin_ref[...])
        xo_ref[...] = x.reshape(ns, tl, D_MODEL)

    _skewed(pl.program_id(0), norm, outputs, body, init, x_refs, bufs, nl)


class _Tiling:
    def __init__(self, bsz, length, ns, tl, skew):
        self.ns, self.tl, self.skew = ns, tl, skew
        self.nl = length // tl
        self.n_tiles = (bsz // ns) * self.nl
        self.grid = (self.n_tiles + 1,) if skew else (self.n_tiles,)

    def body_tile(self, t):
        return jnp.maximum(t - 1, 0) if self.skew else t

    def proj_tile(self, t):
        return jnp.minimum(t, self.n_tiles - 1)

    def x_specs(self, width):
        blk = (self.ns, self.tl, width)
        body = pl.BlockSpec(blk, lambda t: (self.body_tile(t) // self.nl, self.body_tile(t) % self.nl, 0))
        if not self.skew:
            return [body]
        return [pl.BlockSpec(blk, lambda t: (self.proj_tile(t) // self.nl, self.proj_tile(t) % self.nl, 0)), body]

    def tile(self, width):
        return self.x_specs(width)[-1]

    def state(self, tail):
        zeros = (0,) * len(tail)
        return pl.BlockSpec((self.ns,) + tail, lambda t: (self.body_tile(t) // self.nl,) + zeros)

    @staticmethod
    def _window(arr, shape):
        if arr.dtype == BF16 and len(shape) >= 2 and (shape[-1] // LANES) % SUBLANES == 0:
            return shape[:-1] + (shape[-1] + LANES,)
        return shape

    def const(self, arr):
        zeros = (0,) * arr.ndim
        return pl.BlockSpec(self._window(arr, arr.shape), lambda t: zeros, pipeline_mode=pl.Buffered(1))

    def layer(self, arr, layer):
        index = (layer,) + (0,) * (arr.ndim - 1)
        return pl.BlockSpec((None,) + self._window(arr, arr.shape[1:]), lambda t: index,
                            pipeline_mode=pl.Buffered(1))

    def bufs(self, shapes):
        return [pltpu.VMEM(s, d) for _ in range(2) for s, d in shapes] if self.skew else []


def _params():
    return pltpu.CompilerParams(dimension_semantics=("arbitrary",), vmem_limit_bytes=VMEM_LIMIT)


def _mixer0(x, sconv, sssm, w, *, ns, tl, emit_gmv, skew):
    bsz, length, _ = x.shape
    tg = _Tiling(bsz, length, ns, tl, skew)
    tc = min(tl, CHUNK)
    n_gm = min(tl, GM_CHUNK)
    m = ns * tl
    has_state = sconv is not None
    pair_shape = (SSM_HEADS // 2, 2 * SSM_HEAD_DIM, SSM_STATE)
    states = [_header(sconv), sssm.reshape((bsz,) + pair_shape)] if has_state else []
    state_specs = [tg.state((SUBLANES, SSM_CONV_DIM)), tg.state(pair_shape)]
    uvz_cols = 2 * GM_GROUPS * GM_GDIM + SSM_DIM
    assert uvz_cols % SSM_CONV_DIM == 0 and uvz_cols + LANES <= w['win0'].shape[1]
    win_specs = [tg.const(w['nm0']),
                 pl.BlockSpec((D_MODEL, uvz_cols + LANES), lambda t: (0, 0), pipeline_mode=pl.Buffered(1)),
                 pl.BlockSpec((D_MODEL, SSM_CONV_DIM), lambda t: (0, uvz_cols // SSM_CONV_DIM),
                              pipeline_mode=pl.Buffered(1))]
    consts = [w['nm0'], w['win0'], w['win0'], w['wdt'], w['dtb'], w['alog'], w['expd'], w['dskip'], w['ssm_cw'],
              w['ssm_cb'], w['snorm'], w['vgain'], w['gmw'][:, :n_gm, :n_gm], w['gmb'][:n_gm], w['wout0']]
    out_shape = [jax.ShapeDtypeStruct(x.shape, F32),
                 jax.ShapeDtypeStruct((bsz, SUBLANES, SSM_CONV_DIM), F32),
                 jax.ShapeDtypeStruct((bsz,) + pair_shape, F32)]
    out_specs = [tg.tile(D_MODEL)] + state_specs
    if emit_gmv:
        out_shape.append(jax.ShapeDtypeStruct((bsz, length, GM_GROUPS * GM_GDIM), F32))
        out_specs.append(tg.tile(GM_GROUPS * GM_GDIM))
    outs = pl.pallas_call(
        functools.partial(_mixer0_kernel, ns=ns, tl=tl, nl=tg.nl, tc=tc, n_gm=n_gm, emit_gmv=emit_gmv, skew=skew,
                          has_state=has_state),
        grid=tg.grid,
        in_specs=tg.x_specs(D_MODEL) + state_specs[:len(states)] + win_specs
        + [tg.const(c) for c in consts[len(win_specs):]],
        out_specs=out_specs,
        out_shape=out_shape,
        scratch_shapes=[pltpu.VMEM((ns, SSM_CONV_DIM // LANES, SUBLANES + tl, LANES), F32),
                        pltpu.VMEM((m, SSM_DIM), F32), pltpu.VMEM((m, SSM_DIM), F32),
                        pltpu.VMEM((m, SSM_DIM), F32), pltpu.VMEM((m, GM_GROUPS * GM_GDIM), BF16),
                        pltpu.VMEM((ns, SSM_GROUPS, SSM_STATE, GROUP_LANES), F32)]
        + tg.bufs([((m, 2 * GM_GROUPS * GM_GDIM + SSM_DIM), F32), ((m, SSM_CONV_DIM), F32), ((m, DT_LANES), F32)]),
        compiler_params=_params(),
        name="mixer0",
    )(*([x] * len(tg.x_specs(D_MODEL))), *states, *consts)
    conv_o = outs[1][:, SUBLANES - (SSM_CONV - 1):]
    ssm_o = outs[2].reshape(bsz, SSM_HEADS, SSM_HEAD_DIM, SSM_STATE)
    return outs[0], conv_o, ssm_o, (outs[3] if emit_gmv else None)


def _mixer1(x, sret, w, pos0, *, ns, tl, skew):
    bsz, length, _ = x.shape
    tg = _Tiling(bsz, length, ns, tl, skew)
    m = ns * tl
    tabs = _retention_tables(ns if tl < JOINT_ROWS else 1, tl, pos0, length)
    has_state = sret is not None
    state_shape = (RET_HEADS, RET_DK, RET_DV)
    states = [sret] if has_state else []
    consts_a = [w['nm1'], w['win1']]
    consts_b = [tabs['dmat'], tabs['qdec'], tabs['kdec'], tabs['cdec'], w['rnorm'], w['wout1']]
    pos_spec = pl.BlockSpec((tl, RET_DK), lambda t: (tg.body_tile(t) % tg.nl, 0))
    return pl.pallas_call(
        functools.partial(_mixer1_kernel, ns=ns, tl=tl, nl=tg.nl, skew=skew, has_state=has_state),
        grid=tg.grid,
        in_specs=tg.x_specs(D_MODEL) + [tg.state(state_shape)] * len(states)
        + [tg.const(c) for c in consts_a] + [pos_spec, pos_spec]
        + [tg.const(c) for c in consts_b],
        out_specs=[tg.tile(D_MODEL), tg.state(state_shape)],
        out_shape=[jax.ShapeDtypeStruct(x.shape, F32), jax.ShapeDtypeStruct((bsz,) + state_shape, F32)],
        scratch_shapes=[pltpu.VMEM((m, RET_V), BF16)]
        + tg.bufs([((m, 2 * RET_QK), F32), ((m, RET_V), BF16), ((m, RET_V), F32)]),
        compiler_params=_params(),
        name="mixer1",
    )(*([x] * len(tg.x_specs(D_MODEL))), *states, *consts_a, tabs['cos'], tabs['sin'], *consts_b)


def _ffn(x, p, sbuf, w, layer, *, ns, tl, final_norm):
    bsz, length, _ = x.shape
    tg = _Tiling(bsz, length, ns, tl, False)
    has_state = sbuf is not None
    states = [_header(sbuf)] if has_state else []
    stacked = [w['nf'], w['wg'], w['wu'], w['ffn_cw'], w['ffn_cb'], w['wd'], w['npl'], w['wpg'], w['wpp']]
    p_spec = pl.BlockSpec((None, ns, tl, PLE_DIM),
                          lambda t: (layer, tg.body_tile(t) // tg.nl, tg.body_tile(t) % tg.nl, 0))
    x, buf_o = pl.pallas_call(
        functools.partial(_ffn_kernel, ns=ns, tl=tl, nl=tg.nl, final_norm=final_norm, skew=False,
                          has_state=has_state),
        grid=tg.grid,
        in_specs=tg.x_specs(D_MODEL) + [p_spec] + [tg.state((SUBLANES, D_FF))] * len(states)
        + [tg.layer(c, layer) for c in stacked] + [tg.const(w['nfin'])],
        out_specs=[tg.tile(D_MODEL), tg.state((SUBLANES, D_FF))],
        out_shape=[jax.ShapeDtypeStruct(x.shape, F32), jax.ShapeDtypeStruct((bsz, SUBLANES, D_FF), F32)],
        scratch_shapes=[pltpu.VMEM((ns, D_FF // LANES, SUBLANES + tl, LANES), F32)],
        compiler_params=_params(),
        name="ffn%d" % layer,
    )(x, p, *states, *stacked, w['nfin'])
    return x, buf_o[:, SUBLANES - (FFN_CONV - 1):]


def _retention_tables(ns, tl, pos0, length):
    half = RET_DK // 2
    inv = ROPE_BASE ** (-np.arange(half, dtype=np.float64) / half)
    ang = (pos0 + np.arange(length, dtype=np.float64))[:, None] * inv[None]
    cos, sin = np.cos(ang), np.sin(ang)
    lg = np.log(1.0 - 2.0 ** (-5.0 - np.arange(RET_HEADS, dtype=np.float64)))
    m = ns * tl
    t = np.arange(m) % tl
    same_seq = (np.arange(m)[:, None] // tl) == (np.arange(m)[None, :] // tl)
    diff = (t[:, None] - t[None, :]).astype(np.float64)
    dmat = np.where((diff[None] >= 0) & same_seq[None], np.exp(lg[:, None, None] * np.maximum(diff, 0.0)[None]), 0.0)
    qdec = np.exp(lg[None, :] * (t[:, None] + 1.0))
    kdec = np.exp(lg[None, :] * (tl - 1.0 - t)[:, None])
    cdec = np.exp(lg * tl)
    tabs = {
        'cos': np.concatenate([cos, cos], axis=-1),
        'sin': np.concatenate([-sin, sin], axis=-1),
        'dmat': dmat,
        'qdec': np.repeat(qdec, RET_DK, axis=1),
        'kdec': np.repeat(kdec, RET_DK, axis=1),
        'cdec': np.repeat(cdec, RET_DV)[None],
    }
    return {k: jnp.asarray(v, F32) for k, v in tabs.items()}


def _header(buf):
    return jnp.pad(buf, ((0, 0), (SUBLANES - buf.shape[1], 0), (0, 0)))


def _trunk(x, p, sconv, sssm, sret, sffn, pos0, w, *, ns, ns_ret, tl, tl_ssd, tl_ffn, emit_gmv, skew_ret):
    x, conv_o, ssm_o, gmv = _mixer0(x, sconv, sssm, w, ns=ns, tl=tl_ssd, emit_gmv=emit_gmv, skew=False)
    x, ffn0 = _ffn(x, p, None if sffn is None else sffn[0], w, 0, ns=ns, tl=tl_ffn, final_norm=False)
    x, ret_o = _mixer1(x, sret, w, pos0, ns=ns_ret, tl=tl, skew=skew_ret)
    x, ffn1 = _ffn(x, p, None if sffn is None else sffn[1], w, 1, ns=ns, tl=tl_ffn, final_norm=True)
    return x, gmv, conv_o, ssm_o, ret_o, jnp.stack([ffn0, ffn1])


def _prepare_weights(norm_mix, norm_ffn, norm_ple, norm_final, l0_w_in, gm_v_gain, gm_w_s, gm_b_s,
                     ssm_conv_w, ssm_conv_b, ssm_dt_bias, ssm_a_log, ssm_d, ssm_norm, l0_w_out,
                     l1_w_in, ret_norm, l1_w_out, ffn_w_gate, ffn_w_up, ffn_conv_w, ffn_conv_b,
                     ffn_w_down, ple_w_gate, ple_w_proj):
    gm = GM_GROUPS * GM_GDIM
    rep = lambda v: jnp.repeat(v, SSM_HEAD_DIM)[None].astype(F32)
    tile = lambda v: jnp.tile(v, DT_LANES // SSM_HEADS)[None].astype(F32)
    lane = np.arange(DT_LANES)

    return {
        'nm0': norm_mix[0][None], 'nm1': norm_mix[1][None],
        'win0': l0_w_in.astype(BF16),
        'wdt': jnp.tile(l0_w_in[:, 2 * gm + SSM_DIM + SSM_CONV_DIM:], (1, DT_LANES // SSM_HEADS)).astype(BF16),
        'dtb': tile(ssm_dt_bias), 'alog': tile(ssm_a_log), 'dskip': rep(ssm_d),
        'expd': jnp.asarray((lane[:, None] < 3 * SSM_HEADS)
                            & (lane[:, None] % SSM_HEADS == np.arange(SSM_DIM)[None] // SSM_HEAD_DIM), BF16),
        'ssm_cw': ssm_conv_w, 'ssm_cb': ssm_conv_b[None], 'snorm': ssm_norm[None], 'vgain': gm_v_gain[None],
        'gmw': gm_w_s, 'gmb': jnp.repeat(gm_b_s.T, GM_GDIM, axis=1),
        'wout0': l0_w_out.astype(BF16),
        'win1': l1_w_in.astype(BF16), 'rnorm': ret_norm[None], 'wout1': l1_w_out.astype(BF16),
        'nf': norm_ffn[:, None], 'wg': ffn_w_gate.astype(BF16), 'wu': ffn_w_up.astype(BF16),
        'ffn_cw': ffn_conv_w, 'ffn_cb': ffn_conv_b[:, None], 'wd': ffn_w_down.astype(BF16),
        'npl': norm_ple[:, None], 'wpg': ple_w_gate.astype(BF16), 'wpp': ple_w_proj.astype(BF16),
        'nfin': norm_final[None],
    }


def kernel(x_prompt, x_sample, state_ssm_conv, state_ssm, state_ret, state_ffn_conv, p_prompt, p_sample, norm_mix, norm_ffn, norm_ple, norm_final, l0_w_in, gm_v_gain, gm_w_s, gm_b_s, ssm_conv_w, ssm_conv_b, ssm_dt_bias, ssm_a_log, ssm_d, ssm_norm, l0_w_out, l1_w_in, ret_norm, l1_w_out, ffn_w_gate, ffn_w_up, ffn_conv_w, ffn_conv_b, ffn_w_down, ple_w_gate, ple_w_proj):
    w = _prepare_weights(norm_mix, norm_ffn, norm_ple, norm_final, l0_w_in, gm_v_gain, gm_w_s, gm_b_s,
                         ssm_conv_w, ssm_conv_b, ssm_dt_bias, ssm_a_log, ssm_d, ssm_norm, l0_w_out,
                         l1_w_in, ret_norm, l1_w_out, ffn_w_gate, ffn_w_up, ffn_conv_w, ffn_conv_b,
                         ffn_w_down, ple_w_gate, ple_w_proj)
    y_p, _, conv_p, ssm_p, ret_p, ffn_p = _trunk(
        x_prompt, p_prompt, None, None, None, None, 0, w, ns=1, ns_ret=1, tl=256, tl_ssd=512, tl_ffn=1024,
        emit_gmv=False, skew_ret=True)
    y_s, gmv_s, conv_s, ssm_s, ret_s, ffn_s = _trunk(
        x_sample, p_sample, state_ssm_conv, state_ssm, state_ret, state_ffn_conv, PAST_LEN, w,
        ns=8, ns_ret=4, tl=32, tl_ssd=32, tl_ffn=32, emit_gmv=True, skew_ret=False)
    return (y_p, y_s, conv_p, conv_s, ssm_p, ssm_s, ret_p, ret_s, ffn_p, ffn_s, gmv_s)
```

```python
import functools

import jax
import jax.numpy as jnp
import numpy as np
from jax import lax
from jax.experimental import pallas as pl
from jax.experimental.pallas import tpu as pltpu

F32 = jnp.float32
BF16 = jnp.bfloat16

D_MODEL = 1024
DEPTH = 2
PAST_LEN = 1024
CHUNK = 64
PLE_DIM = 256
EPS = 1e-6
GM_CHUNK = 128
GM_GROUPS = 8
GM_GDIM = 128
SSM_HEADS = 16
SSM_HEAD_DIM = 64
SSM_DIM = 1024
SSM_GROUPS = 2
SSM_STATE = 128
SSM_CONV = 4
SSM_CONV_DIM = 1536
RET_HEADS = 8
RET_DK = 128
RET_DV = 256
RET_QK = 1024
RET_V = 2048
ROPE_BASE = 10000.0
D_FF = 2816
FFN_CONV = 3

SUBLANES = 8
LANES = 128
GROUP_LANES = SSM_HEADS // SSM_GROUPS * SSM_HEAD_DIM
DT_LANES = 128
VMEM_LIMIT = 62 * 1024 * 1024


def _dot(a, b):
    return jnp.dot(a, b, preferred_element_type=F32)


def _dot_nt(a, b):
    return lax.dot_general(a, b, (((1,), (1,)), ((), ())), preferred_element_type=F32)


def _dot_tn(a, b):
    return lax.dot_general(a, b, (((0,), (0,)), ((), ())), preferred_element_type=F32)


def _rms(x, g):
    return x * lax.rsqrt(jnp.mean(x * x, axis=-1, keepdims=True) + EPS) * g


def _silu(x):
    return x * jax.nn.sigmoid(x)


def _softplus(x):
    return jnp.maximum(x, 0.0) + jnp.log1p(jnp.exp(-jnp.abs(x)))


def _pad_store(pad_ref, start, value):
    rows = value.shape[1]
    for j in range(pad_ref.shape[1]):
        pad_ref[:, j, start:start + rows, :] = value[:, :, j * LANES:(j + 1) * LANES]


def _pad_load(pad_ref, start, rows, j0=0, j1=None):
    j1 = pad_ref.shape[1] if j1 is None else j1
    return jnp.concatenate([pad_ref[:, j, start:start + rows, :] for j in range(j0, j1)], axis=-1)


def _causal_conv(pad_ref, w_ref, b_ref, tl, j0=0, j1=None):
    j1 = pad_ref.shape[1] if j1 is None else j1
    width = w_ref.shape[0]
    lanes = slice(j0 * LANES, j1 * LANES)
    y = b_ref[:, lanes][None]
    for k in range(width):
        y = y + w_ref[k:k + 1, lanes][None] * _pad_load(pad_ref, SUBLANES - (width - 1) + k, tl, j0, j1)
    return y


class _Cols:
    def __init__(self, src):
        self.src = src

    def cols(self, a, b):
        return self.src[:, a:b]


PROJ_BLOCK = 512
LATE_BLOCK = 512
JOINT_ROWS = 128
BAND_ROWS = 256


def _proj_spec(outputs):
    spec = []
    for o, (w_ref, first, n_cols, _) in enumerate(outputs):
        for c in range(0, n_cols, PROJ_BLOCK):
            spec.append((o, w_ref, first + c, c, min(PROJ_BLOCK, n_cols - c)))
    return spec


def _skewed(t, norm_fn, outputs, body_fn, init_fn, x_refs, bufs, nl, final_fn=None):
    spec = _proj_spec(outputs)
    if bufs is None:
        (x_ref,) = x_refs

        @pl.when(t % nl == 0)
        def _():
            init_fn()

        x = x_ref[...].reshape(-1, D_MODEL)
        h = norm_fn(x)
        parts = [[] for _ in outputs]
        for o, w_ref, wc, _, n in spec:
            parts[o].append(_dot(h, w_ref[:, wc:wc + n]).astype(outputs[o][3]))
        body_fn(x, tuple(_Cols(jnp.concatenate(p, axis=1) if len(p) > 1 else p[0]) for p in parts), lambda n=1: None,
                h)
        if final_fn is not None:
            @pl.when(t % nl == nl - 1)
            def _():
                final_fn()
        return

    xa_ref, xb_ref = x_refs

    @pl.when(t == 0)
    def _():
        for r in bufs[1]:
            r[...] = jnp.zeros(r.shape, r.dtype)

    @pl.when(jnp.maximum(t - 1, 0) % nl == 0)
    def _():
        init_fn()

    for parity in range(2):
        @pl.when(t % 2 == parity)
        def _(parity=parity):
            h = norm_fn(xa_ref[...].reshape(-1, D_MODEL))
            pieces = iter(spec)

            def tick(n=1):
                for _ in range(n):
                    piece = next(pieces, None)
                    if piece is not None:
                        o, w_ref, wc, oc, width = piece
                        bufs[parity][o][:, oc:oc + width] = _dot(h, w_ref[:, wc:wc + width]).astype(outputs[o][3])

            body_fn(xb_ref[...].reshape(-1, D_MODEL), tuple(_Cols(r) for r in bufs[1 - parity]), tick, None)
            tick(len(spec))

    if final_fn is not None:
        @pl.when(jnp.maximum(t - 1, 0) % nl == nl - 1)
        def _():
            final_fn()


def _split_refs(refs, n_x, n_in, n_out, n_scratch):
    refs = list(refs)
    x_refs, refs = refs[:n_x], refs[n_x:]
    ins, refs = refs[:n_in], refs[n_in:]
    outs, refs = refs[:n_out], refs[n_out:]
    scratch, refs = refs[:n_scratch], refs[n_scratch:]
    bufs = (refs[:len(refs) // 2], refs[len(refs) // 2:]) if refs else None
    return x_refs, ins, outs, scratch, bufs


def _mixer0_kernel(*refs, ns, tl, nl, tc, n_gm, emit_gmv, skew, has_state):
    n_state = 2 if has_state else 0
    x_refs, ins, outs, scratch, bufs = _split_refs(refs, 2 if skew else 1, n_state + 15, 4 if emit_gmv else 3, 6)
    sconv_ref, sssm_ref = ins[:n_state] if has_state else (None, None)
    (nm_ref, wuvz_ref, wxbc_ref, wdt_ref, dtb_ref, alog_ref, expd_ref, dskip_ref, cw_ref,
     cb_ref, snorm_ref, vgain_ref, gmw_ref, gmb_ref, wout_ref) = ins[n_state:]
    xo_ref, convo_ref, sso_ref = outs[:3]
    gmv_ref = outs[3] if emit_gmv else None
    xpad_ref, xs_ref, cs_ref, y_ref, mix_ref, ht_ref = scratch
    pairs_per_group = SSM_HEADS // SSM_GROUPS // 2
    nb = max(1, ns * tl // BAND_ROWS)
    assert ns == 1 or nb == 1
    tl = tl // nb
    m = ns * tl
    nck = tl // tc
    nchunks = m // tc
    gm = GM_GROUPS * GM_GDIM

    def init():
        if not has_state:
            for j in range(xpad_ref.shape[1]):
                xpad_ref[:, j, 0:SUBLANES, :] = jnp.zeros((ns, SUBLANES, LANES), F32)
            ht_ref[...] = jnp.zeros(ht_ref.shape, F32)
            return
        _pad_store(xpad_ref, 0, sconv_ref[...])
        for s in range(ns):
            for pr in range(SSM_HEADS // 2):
                g, q = divmod(pr, pairs_per_group)
                ht_ref[s, g, :, q * 2 * SSM_HEAD_DIM:(q + 1) * 2 * SSM_HEAD_DIM] = sssm_ref[s, pr].T

    def final():
        for s in range(ns):
            for pr in range(SSM_HEADS // 2):
                g, q = divmod(pr, pairs_per_group)
                sso_ref[s, pr] = ht_ref[s, g, :, q * 2 * SSM_HEAD_DIM:(q + 1) * 2 * SSM_HEAD_DIM].T

    def norm(x):
        return _rms(x, nm_ref[...]).astype(BF16)

    assert not skew
    outputs = [(wuvz_ref, 0, 2 * gm, F32), (wxbc_ref, 0, SSM_CONV_DIM, F32), (wdt_ref, 0, DT_LANES, F32)]

    def body(x, projected, tick, h):
        for b in range(nb):
            rows = slice(b * m, (b + 1) * m)
            band_body(x[rows], tuple(_Cols(p.src[rows]) for p in projected), h[rows], b)

    def band_body(x, projected, h, b_idx):
        uvz, xbc, dtr = projected

        z_parts, xg_parts = {}, {}

        def z_block(c0):
            z_parts[c0] = _dot(h, wuvz_ref[:, 2 * gm + c0:2 * gm + c0 + LATE_BLOCK])

        def out_gm_block(c0):
            xg_parts[c0] = x[:, c0:c0 + LATE_BLOCK] + _dot(mix_ref[:, :gm], wout_ref[:gm, c0:c0 + LATE_BLOCK])

        late = [functools.partial(f, c0) for c0 in range(0, SSM_DIM, LATE_BLOCK) for f in (out_gm_block, z_block)]

        def emit_late():
            if late:
                late.pop(0)()

        ri = lax.broadcasted_iota(jnp.int32, (n_gm, n_gm), 0)
        ci = lax.broadcasted_iota(jnp.int32, (n_gm, n_gm), 1)
        gm_mask = (ci // CHUNK) <= (ri // CHUNK)
        for g in range(GM_GROUPS):
            lanes = slice(g * GM_GDIM, (g + 1) * GM_GDIM)
            u_g = jax.nn.gelu(uvz.cols(g * GM_GDIM, (g + 1) * GM_GDIM))
            v_g = jax.nn.gelu(uvz.cols(gm + g * GM_GDIM, gm + (g + 1) * GM_GDIM))
            v_g = v_g * lax.rsqrt(jnp.mean(v_g * v_g, axis=-1, keepdims=True) + EPS) * vgain_ref[:, lanes]
            if emit_gmv:
                gmv_ref[:, :, lanes] = v_g.reshape(ns, tl, GM_GDIM)
            w_g = jnp.where(gm_mask, gmw_ref[g], 0.0).astype(BF16)
            v_gb = v_g.astype(BF16)
            for c in range(m // n_gm):
                rows = slice(c * n_gm, (c + 1) * n_gm)
                s = _dot(w_g, v_gb[rows]) + gmb_ref[:, lanes]
                mix_ref[rows, lanes] = (u_g[rows] * s).astype(BF16)

        for j in range(SSM_CONV_DIM // LANES):
            xpad_ref[:, j, SUBLANES:, :] = xbc.cols(j * LANES, (j + 1) * LANES).reshape(ns, tl, LANES)
        emit_late()
        xc = _silu(_causal_conv(xpad_ref, cw_ref, cb_ref, tl)).reshape(m, SSM_CONV_DIM)
        tail = _pad_load(xpad_ref, tl, SUBLANES)
        _pad_store(xpad_ref, 0, tail)
        convo_ref[...] = tail
        emit_late()
        xh = xc[:, :SSM_DIM]
        bm = xc[:, SSM_DIM:SSM_DIM + SSM_GROUPS * SSM_STATE].astype(BF16)
        cm = xc[:, SSM_DIM + SSM_GROUPS * SSM_STATE:].astype(BF16)

        dt_n = _softplus(dtr.cols(0, DT_LANES) + dtb_ref[...])
        da_n = dt_n * (-jnp.exp(alog_ref[...]))
        emit_late()

        def split3(v):
            hi = v.astype(BF16).astype(F32)
            r1 = v - hi
            mid = r1.astype(BF16).astype(F32)
            return hi, mid, r1 - mid

        lane_grp = lax.broadcasted_iota(jnp.int32, (m, DT_LANES), 1) // SSM_HEADS

        def expand(v):
            hi, mid, lo = split3(v)
            parts = jnp.where(lane_grp == 0, hi, jnp.where(lane_grp == 1, mid, lo))
            return _dot(parts.astype(BF16), expd_ref[:, :SSM_DIM])

        band = min(m, BAND_ROWS)
        ri = lax.broadcasted_iota(jnp.int32, (band, band), 0)
        ci = lax.broadcasted_iota(jnp.int32, (band, band), 1)
        tri = jnp.where(((ri // tc) == (ci // tc)) & (ci <= ri), 1.0, 0.0).astype(BF16)
        da3 = jnp.concatenate(split3(da_n), axis=1).astype(BF16)
        cs3 = [_dot(tri, da3[b * band:(b + 1) * band]) for b in range(m // band)]
        cs3 = jnp.concatenate(cs3, axis=0) if len(cs3) > 1 else cs3[0]
        cs_n = cs3[:, :DT_LANES] + cs3[:, DT_LANES:2 * DT_LANES] + cs3[:, 2 * DT_LANES:]
        cs_ref[...] = expand(cs_n)
        dtx = expand(dt_n)
        xs_ref[...] = xh * dtx
        y_ref[...] = dskip_ref[...] * xh
        emit_late()

        cpb = band // tc

        def dup_rows(b_g):
            parts = []
            for k in range(cpb):
                blk = b_g[k * tc:(k + 1) * tc]
                if tc < CHUNK:
                    blk = jnp.concatenate([blk, jnp.zeros((CHUNK - tc, SSM_STATE), BF16)], axis=0)
                parts += [blk, blk]
            return jnp.concatenate(parts, axis=0)

        cb2 = [[_dot_nt(cm[b * band:(b + 1) * band, g * SSM_STATE:(g + 1) * SSM_STATE],
                        dup_rows(bm[b * band:(b + 1) * band, g * SSM_STATE:(g + 1) * SSM_STATE]))
                for g in range(SSM_GROUPS)] for b in range(m // band)]

        t_i = lax.broadcasted_iota(jnp.int32, (tc, SSM_DIM), 0)
        s_i = lax.broadcasted_iota(jnp.int32, (tc, SSM_DIM), 1) % SSM_HEAD_DIM
        causal = s_i <= t_i
        diag = s_i == t_i
        quad_head = lax.broadcasted_iota(jnp.int32, (tc, 4 * SSM_HEAD_DIM), 1) // SSM_HEAD_DIM
        heads_per_group = SSM_HEADS // SSM_GROUPS

        for c in range(nchunks):
            seq = c // nck
            rows = slice(c * tc, (c + 1) * tc)
            cs_c = cs_ref[rows, :]
            cs_row = jnp.sum(jnp.where(diag, cs_c, 0.0), axis=0, keepdims=True)
            cs_last = cs_c[tc - 1:tc, :]
            lmat = jnp.exp(jnp.where(causal, cs_c - cs_row, -1e30))
            b, k = divmod(c, cpb)
            cb_c = jnp.concatenate(
                [cb2[b][g][k * tc:(k + 1) * tc, k * 2 * CHUNK:(k + 1) * 2 * CHUNK] for g in range(SSM_GROUPS)
                 for _ in range(heads_per_group // 2)], axis=1)
            mm = (lmat * cb_c).astype(BF16)
            xs_c = xs_ref[rows, :]
            y_parts = []
            for q in range(SSM_HEADS // 4):
                ql = slice(q * 4 * SSM_HEAD_DIM, (q + 1) * 4 * SSM_HEAD_DIM)
                xq = xs_c[:, ql]
                blocks = []
                for hh in range(4):
                    blk = jnp.where(quad_head == hh, xq, 0.0).astype(BF16)
                    if tc < CHUNK:
                        blk = jnp.concatenate([blk, jnp.zeros((CHUNK - tc, 4 * SSM_HEAD_DIM), BF16)], axis=0)
                    blocks.append(blk)
                y_parts.append(_dot(mm[:, ql], jnp.concatenate(blocks, axis=0)))
            y_c = jnp.concatenate(y_parts, axis=1)
            ecs = jnp.exp(cs_c)
            xsd = (xs_c * jnp.exp(cs_last - cs_c)).astype(BF16)
            e_last = ecs[tc - 1:tc, :]
            y_off = []
            for g in range(SSM_GROUPS):
                gl = slice(g * GROUP_LANES, (g + 1) * GROUP_LANES)
                sl = slice(g * SSM_STATE, (g + 1) * SSM_STATE)
                h_prev = ht_ref[seq, g]
                y_off.append(_dot(cm[rows, sl], h_prev.astype(BF16)))
                ht_ref[seq, g] = h_prev * e_last[:, gl] + _dot_tn(bm[rows, sl], xsd[:, gl])
            y_ref[rows, :] = y_ref[rows, :] + y_c + jnp.concatenate(y_off, axis=1) * ecs
        while late:
            emit_late()

        z = jnp.concatenate([z_parts[c0] for c0 in sorted(z_parts)], axis=1)
        x_gm = jnp.concatenate([xg_parts[c0] for c0 in sorted(xg_parts)], axis=1)
        yb = _rms(y_ref[...] * _silu(z), snorm_ref[...])
        out = x_gm + _dot(yb.astype(BF16), wout_ref[gm:, :D_MODEL])
        if nb == 1:
            xo_ref[...] = out.reshape(ns, tl, D_MODEL)
        else:
            xo_ref[0, b_idx * m:(b_idx + 1) * m, :] = out

    _skewed(pl.program_id(0), norm, outputs, body, init, x_refs, bufs, nl, final)


def _mixer1_kernel(*refs, ns, tl, nl, skew, has_state):
    n_state = 1 if has_state else 0
    x_refs, ins, outs, scratch, bufs = _split_refs(refs, 2 if skew else 1, n_state + 10, 2, 1)
    sret_ref = ins[0] if has_state else None
    (nm_ref, win_ref, cos_ref, sin_ref, dmat_ref, qdec_ref, kdec_ref, cdec_ref, rnorm_ref,
     wout_ref) = ins[n_state:]
    xo_ref, s_ref = outs
    (mix_ref,) = scratch
    m = ns * tl

    def init():
        s_ref[...] = sret_ref[...] if has_state else jnp.zeros(s_ref.shape, F32)

    def norm(x):
        return _rms(x, nm_ref[...]).astype(BF16)

    outputs = [(win_ref, 0, 2 * RET_QK, F32), (win_ref, 2 * RET_QK, RET_V, BF16), (win_ref, 2 * RET_QK + RET_V, RET_V, F32)]
    n_ticks = RET_HEADS + RET_HEADS // 2
    per_tick = -(-len(_proj_spec(outputs)) // n_ticks)

    def body(x, projected, tick, _h):
        qk, vv, gg = projected
        cos = jnp.concatenate([cos_ref[...]] * ns, axis=0) if ns > 1 else cos_ref[...]
        sin = jnp.concatenate([sin_ref[...]] * ns, axis=0) if ns > 1 else sin_ref[...]
        nseg = m // dmat_ref.shape[1]
        seg = m // nseg
        qdec = jnp.concatenate([qdec_ref[...]] * nseg, axis=0) if nseg > 1 else qdec_ref[...]
        kdec = jnp.concatenate([kdec_ref[...]] * nseg, axis=0) if nseg > 1 else kdec_ref[...]
        scale = RET_DK ** -0.5
        for hd in range(RET_HEADS):
            kl = slice(hd * RET_DK, (hd + 1) * RET_DK)
            vl = slice(hd * RET_DV, (hd + 1) * RET_DV)
            q = qk.cols(hd * RET_DK, (hd + 1) * RET_DK)
            k = qk.cols(RET_QK + hd * RET_DK, RET_QK + (hd + 1) * RET_DK)
            v = vv.cols(hd * RET_DV, (hd + 1) * RET_DV)
            gate = gg.cols(hd * RET_DV, (hd + 1) * RET_DV)
            q = q * cos + pltpu.roll(q, RET_DK // 2, axis=1) * sin
            k = (k * cos + pltpu.roll(k, RET_DK // 2, axis=1) * sin) * scale
            qb, kb = q.astype(BF16), k.astype(BF16)
            y_intra = []
            for sg in range(nseg):
                rows = slice(sg * seg, (sg + 1) * seg)
                sc = _dot_nt(qb[rows], kb[rows]) * dmat_ref[hd]
                y_intra.append(_dot(sc.astype(BF16), v[rows]))
            y = jnp.concatenate(y_intra, axis=0) if nseg > 1 else y_intra[0]
            qd = (q * qdec[:, kl]).astype(BF16)
            kd = (k * kdec[:, kl]).astype(BF16)
            y_inter = []
            for s in range(ns):
                rows = slice(s * tl, (s + 1) * tl)
                s_prev = s_ref[s, hd]
                y_inter.append(_dot(qd[rows], s_prev.astype(BF16)))
                s_ref[s, hd] = s_prev * cdec_ref[:, vl] + _dot_tn(kd[rows], v[rows])
            y = y + (jnp.concatenate(y_inter, axis=0) if ns > 1 else y_inter[0])
            if hd % 2 == 0:
                tick(per_tick)
            mu = jnp.mean(y, axis=-1, keepdims=True)
            yc = y - mu
            var = jnp.mean(yc * yc, axis=-1, keepdims=True)
            yn = yc * lax.rsqrt(var + EPS) * rnorm_ref[:, vl]
            mix_ref[:, vl] = (_silu(gate) * yn).astype(BF16)
            tick(per_tick)
            if hd % 2 == 1:
                pair = slice((hd - 1) * RET_DV, (hd + 1) * RET_DV)
                x = x + _dot(mix_ref[:, pair], wout_ref[pair, :D_MODEL])
        xo_ref[...] = x.reshape(ns, tl, D_MODEL)

    _skewed(pl.program_id(0), norm, outputs, body, init, x_refs, bufs, nl)


def _ffn_kernel(*refs, ns, tl, nl, final_norm, skew, has_state):
    n_state = 1 if has_state else 0
    x_refs, ins, outs, scratch, bufs = _split_refs(refs, 2 if skew else 1, n_state + 11, 2, 1)
    p_ref = ins[0]
    sbuf_ref = ins[1] if has_state else None
    (nf_ref, wg_ref, wu_ref, cw_ref, cb_ref, wd_ref, npl_ref, wpg_ref, wpp_ref, nfin_ref) = ins[1 + n_state:]
    xo_ref, bufo_ref = outs
    (gpad_ref,) = scratch
    m = ns * tl

    def init():
        if has_state:
            _pad_store(gpad_ref, 0, sbuf_ref[...])
        else:
            for j in range(gpad_ref.shape[1]):
                gpad_ref[:, j, 0:SUBLANES, :] = jnp.zeros((ns, SUBLANES, LANES), F32)

    def norm(x):
        return _rms(x, nf_ref[...]).astype(BF16)

    outputs = []

    def body(x, projected, tick, h):
        pp = _dot(p_ref[...].reshape(m, PLE_DIM).astype(BF16), wpp_ref[:, :D_MODEL])
        _pad_store(gpad_ref, SUBLANES, _dot(h, wg_ref[...]).reshape(ns, tl, D_FF))
        gate = _causal_conv(gpad_ref, cw_ref, cb_ref, tl).reshape(m, D_FF)
        tail = _pad_load(gpad_ref, tl, SUBLANES)
        _pad_store(gpad_ref, 0, tail)
        bufo_ref[...] = tail
        act = (_silu(gate) * _dot(h, wu_ref[...])).astype(BF16)
        x = x + _dot(act, wd_ref[:, :D_MODEL])
        pgate = jax.nn.sigmoid(_dot(_rms(x, npl_ref[...]).astype(BF16), wpg_ref[:, :D_MODEL]))
        x = x + pgate * pp
        if final_norm:
            x = _rms(x, nfin_ref[...])
        xo_ref[...] = x.reshape(ns, tl, D_MODEL)

    _skewed(pl.program_id(0), norm, outputs, body, init, x_refs, bufs, nl)


class _Tiling:
    def __init__(self, bsz, length, ns, tl, skew):
        self.ns, self.tl, self.skew = ns, tl, skew
        self.nl = length // tl
        self.n_tiles = (bsz // ns) * self.nl
        self.grid = (self.n_tiles + 1,) if skew else (self.n_tiles,)

    def body_tile(self, t):
        return jnp.maximum(t - 1, 0) if self.skew else t

    def proj_tile(self, t):
        return jnp.minimum(t, self.n_tiles - 1)

    def x_specs(self, width):
        blk = (self.ns, self.tl, width)
        body = pl.BlockSpec(blk, lambda t: (self.body_tile(t) // self.nl, self.body_tile(t) % self.nl, 0))
        if not self.skew:
            return [body]
        return [pl.BlockSpec(blk, lambda t: (self.proj_tile(t) // self.nl, self.proj_tile(t) % self.nl, 0)), body]

    def tile(self, width):
        return self.x_specs(width)[-1]

    def state(self, tail):
        zeros = (0,) * len(tail)
        return pl.BlockSpec((self.ns,) + tail, lambda t: (self.body_tile(t) // self.nl,) + zeros)

    @staticmethod
    def _window(arr, shape):
        if arr.dtype == BF16 and len(shape) >= 2 and (shape[-1] // LANES) % SUBLANES == 0:
            return shape[:-1] + (shape[-1] + LANES,)
        return shape

    def const(self, arr):
        zeros = (0,) * arr.ndim
        return pl.BlockSpec(self._window(arr, arr.shape), lambda t: zeros, pipeline_mode=pl.Buffered(1))

    def layer(self, arr, layer):
        index = (layer,) + (0,) * (arr.ndim - 1)
        return pl.BlockSpec((None,) + self._window(arr, arr.shape[1:]), lambda t: index,
                            pipeline_mode=pl.Buffered(1))

    def bufs(self, shapes):
        return [pltpu.VMEM(s, d) for _ in range(2) for s, d in shapes] if self.skew else []


def _params():
    return pltpu.CompilerParams(dimension_semantics=("arbitrary",), vmem_limit_bytes=VMEM_LIMIT)


def _mixer0(x, sconv, sssm, w, *, ns, tl, emit_gmv, skew):
    bsz, length, _ = x.shape
    tg = _Tiling(bsz, length, ns, tl, skew)
    tc = min(tl, CHUNK)
    n_gm = min(tl, GM_CHUNK)
    m = ns * tl
    nb = max(1, m // BAND_ROWS)
    has_state = sconv is not None
    pair_shape = (SSM_HEADS // 2, 2 * SSM_HEAD_DIM, SSM_STATE)
    states = [_header(sconv), sssm.reshape((bsz,) + pair_shape)] if has_state else []
    state_specs = [tg.state((SUBLANES, SSM_CONV_DIM)), tg.state(pair_shape)]
    uvz_cols = 2 * GM_GROUPS * GM_GDIM + SSM_DIM
    assert uvz_cols % SSM_CONV_DIM == 0 and uvz_cols + LANES <= w['win0'].shape[1]
    win_specs = [tg.const(w['nm0']),
                 pl.BlockSpec((D_MODEL, uvz_cols + LANES), lambda t: (0, 0), pipeline_mode=pl.Buffered(1)),
                 pl.BlockSpec((D_MODEL, SSM_CONV_DIM), lambda t: (0, uvz_cols // SSM_CONV_DIM),
                              pipeline_mode=pl.Buffered(1))]
    consts = [w['nm0'], w['win0'], w['win0'], w['wdt'], w['dtb'], w['alog'], w['expd'], w['dskip'], w['ssm_cw'],
              w['ssm_cb'], w['snorm'], w['vgain'], w['gmw'][:, :n_gm, :n_gm], w['gmb'][:n_gm], w['wout0']]
    out_shape = [jax.ShapeDtypeStruct(x.shape, F32),
                 jax.ShapeDtypeStruct((bsz, SUBLANES, SSM_CONV_DIM), F32),
                 jax.ShapeDtypeStruct((bsz,) + pair_shape, F32)]
    out_specs = [tg.tile(D_MODEL)] + state_specs
    if emit_gmv:
        out_shape.append(jax.ShapeDtypeStruct((bsz, length, GM_GROUPS * GM_GDIM), F32))
        out_specs.append(tg.tile(GM_GROUPS * GM_GDIM))
    outs = pl.pallas_call(
        functools.partial(_mixer0_kernel, ns=ns, tl=tl, nl=tg.nl, tc=tc, n_gm=n_gm, emit_gmv=emit_gmv, skew=skew,
                          has_state=has_state),
        grid=tg.grid,
        in_specs=tg.x_specs(D_MODEL) + state_specs[:len(states)] + win_specs
        + [tg.const(c) for c in consts[len(win_specs):]],
        out_specs=out_specs,
        out_shape=out_shape,
        scratch_shapes=[pltpu.VMEM((ns, SSM_CONV_DIM // LANES, SUBLANES + tl // nb, LANES), F32),
                        pltpu.VMEM((m // nb, SSM_DIM), F32), pltpu.VMEM((m // nb, SSM_DIM), F32),
                        pltpu.VMEM((m // nb, SSM_DIM), F32), pltpu.VMEM((m // nb, GM_GROUPS * GM_GDIM), BF16),
                        pltpu.VMEM((ns, SSM_GROUPS, SSM_STATE, GROUP_LANES), F32)]
        + tg.bufs([((m, 2 * GM_GROUPS * GM_GDIM + SSM_DIM), F32), ((m, SSM_CONV_DIM), F32), ((m, DT_LANES), F32)]),
        compiler_params=_params(),
        name="mixer0",
    )(*([x] * len(tg.x_specs(D_MODEL))), *states, *consts)
    conv_o = outs[1][:, SUBLANES - (SSM_CONV - 1):]
    ssm_o = outs[2].reshape(bsz, SSM_HEADS, SSM_HEAD_DIM, SSM_STATE)
    return outs[0], conv_o, ssm_o, (outs[3] if emit_gmv else None)


def _mixer1(x, sret, w, pos0, *, ns, tl, skew):
    bsz, length, _ = x.shape
    tg = _Tiling(bsz, length, ns, tl, skew)
    m = ns * tl
    tabs = _retention_tables(ns if tl < JOINT_ROWS else 1, tl, pos0, length)
    has_state = sret is not None
    state_shape = (RET_HEADS, RET_DK, RET_DV)
    states = [sret] if has_state else []
    consts_a = [w['nm1'], w['win1']]
    consts_b = [tabs['dmat'], tabs['qdec'], tabs['kdec'], tabs['cdec'], w['rnorm'], w['wout1']]
    pos_spec = pl.BlockSpec((tl, RET_DK), lambda t: (tg.body_tile(t) % tg.nl, 0))
    return pl.pallas_call(
        functools.partial(_mixer1_kernel, ns=ns, tl=tl, nl=tg.nl, skew=skew, has_state=has_state),
        grid=tg.grid,
        in_specs=tg.x_specs(D_MODEL) + [tg.state(state_shape)] * len(states)
        + [tg.const(c) for c in consts_a] + [pos_spec, pos_spec]
        + [tg.const(c) for c in consts_b],
        out_specs=[tg.tile(D_MODEL), tg.state(state_shape)],
        out_shape=[jax.ShapeDtypeStruct(x.shape, F32), jax.ShapeDtypeStruct((bsz,) + state_shape, F32)],
        scratch_shapes=[pltpu.VMEM((m, RET_V), BF16)]
        + tg.bufs([((m, 2 * RET_QK), F32), ((m, RET_V), BF16), ((m, RET_V), F32)]),
        compiler_params=_params(),
        name="mixer1",
    )(*([x] * len(tg.x_specs(D_MODEL))), *states, *consts_a, tabs['cos'], tabs['sin'], *consts_b)


def _ffn(x, p, sbuf, w, layer, *, ns, tl, final_norm):
    bsz, length, _ = x.shape
    tg = _Tiling(bsz, length, ns, tl, False)
    has_state = sbuf is not None
    states = [_header(sbuf)] if has_state else []
    stacked = [w['nf'], w['wg'], w['wu'], w['ffn_cw'], w['ffn_cb'], w['wd'], w['npl'], w['wpg'], w['wpp']]
    p_spec = pl.BlockSpec((None, ns, tl, PLE_DIM),
                          lambda t: (layer, tg.body_tile(t) // tg.nl, tg.body_tile(t) % tg.nl, 0))
    x, buf_o = pl.pallas_call(
        functools.partial(_ffn_kernel, ns=ns, tl=tl, nl=tg.nl, final_norm=final_norm, skew=False,
                          has_state=has_state),
        grid=tg.grid,
        in_specs=tg.x_specs(D_MODEL) + [p_spec] + [tg.state((SUBLANES, D_FF))] * len(states)
        + [tg.layer(c, layer) for c in stacked] + [tg.const(w['nfin'])],
        out_specs=[tg.tile(D_MODEL), tg.state((SUBLANES, D_FF))],
        out_shape=[jax.ShapeDtypeStruct(x.shape, F32), jax.ShapeDtypeStruct((bsz, SUBLANES, D_FF), F32)],
        scratch_shapes=[pltpu.VMEM((ns, D_FF // LANES, SUBLANES + tl, LANES), F32)],
        compiler_params=_params(),
        name="ffn%d" % layer,
    )(x, p, *states, *stacked, w['nfin'])
    return x, buf_o[:, SUBLANES - (FFN_CONV - 1):]


def _retention_tables(ns, tl, pos0, length):
    half = RET_DK // 2
    inv = ROPE_BASE ** (-np.arange(half, dtype=np.float64) / half)
    ang = (pos0 + np.arange(length, dtype=np.float64))[:, None] * inv[None]
    cos, sin = np.cos(ang), np.sin(ang)
    lg = np.log(1.0 - 2.0 ** (-5.0 - np.arange(RET_HEADS, dtype=np.float64)))
    m = ns * tl
    t = np.arange(m) % tl
    same_seq = (np.arange(m)[:, None] // tl) == (np.arange(m)[None, :] // tl)
    diff = (t[:, None] - t[None, :]).astype(np.float64)
    dmat = np.where((diff[None] >= 0) & same_seq[None], np.exp(lg[:, None, None] * np.maximum(diff, 0.0)[None]), 0.0)
    qdec = np.exp(lg[None, :] * (t[:, None] + 1.0))
    kdec = np.exp(lg[None, :] * (tl - 1.0 - t)[:, None])
    cdec = np.exp(lg * tl)
    tabs = {
        'cos': np.concatenate([cos, cos], axis=-1),
        'sin': np.concatenate([-sin, sin], axis=-1),
        'dmat': dmat,
        'qdec': np.repeat(qdec, RET_DK, axis=1),
        'kdec': np.repeat(kdec, RET_DK, axis=1),
        'cdec': np.repeat(cdec, RET_DV)[None],
    }
    return {k: jnp.asarray(v, F32) for k, v in tabs.items()}


def _header(buf):
    return jnp.pad(buf, ((0, 0), (SUBLANES - buf.shape[1], 0), (0, 0)))


def _trunk(x, p, sconv, sssm, sret, sffn, pos0, w, *, ns, ns_ret, tl, tl_ssd, tl_ffn, emit_gmv, skew_ret):
    x, conv_o, ssm_o, gmv = _mixer0(x, sconv, sssm, w, ns=ns, tl=tl_ssd, emit_gmv=emit_gmv, skew=False)
    x, ffn0 = _ffn(x, p, None if sffn is None else sffn[0], w, 0, ns=ns, tl=tl_ffn, final_norm=False)
    x, ret_o = _mixer1(x, sret, w, pos0, ns=ns_ret, tl=tl, skew=skew_ret)
    x, ffn1 = _ffn(x, p, None if sffn is None else sffn[1], w, 1, ns=ns, tl=tl_ffn, final_norm=True)
    return x, gmv, conv_o, ssm_o, ret_o, jnp.stack([ffn0, ffn1])


def _prepare_weights(norm_mix, norm_ffn, norm_ple, norm_final, l0_w_in, gm_v_gain, gm_w_s, gm_b_s,
                     ssm_conv_w, ssm_conv_b, ssm_dt_bias, ssm_a_log, ssm_d, ssm_norm, l0_w_out,
                     l1_w_in, ret_norm, l1_w_out, ffn_w_gate, ffn_w_up, ffn_conv_w, ffn_conv_b,
                     ffn_w_down, ple_w_gate, ple_w_proj):
    gm = GM_GROUPS * GM_GDIM
    rep = lambda v: jnp.repeat(v, SSM_HEAD_DIM)[None].astype(F32)
    tile = lambda v: jnp.tile(v, DT_LANES // SSM_HEADS)[None].astype(F32)
    lane = np.arange(DT_LANES)

    return {
        'nm0': norm_mix[0][None], 'nm1': norm_mix[1][None],
        'win0': l0_w_in.astype(BF16),
        'wdt': jnp.tile(l0_w_in[:, 2 * gm + SSM_DIM + SSM_CONV_DIM:], (1, DT_LANES // SSM_HEADS)).astype(BF16),
        'dtb': tile(ssm_dt_bias), 'alog': tile(ssm_a_log), 'dskip': rep(ssm_d),
        'expd': jnp.asarray((lane[:, None] < 3 * SSM_HEADS)
                            & (lane[:, None] % SSM_HEADS == np.arange(SSM_DIM)[None] // SSM_HEAD_DIM), BF16),
        'ssm_cw': ssm_conv_w, 'ssm_cb': ssm_conv_b[None], 'snorm': ssm_norm[None], 'vgain': gm_v_gain[None],
        'gmw': gm_w_s, 'gmb': jnp.repeat(gm_b_s.T, GM_GDIM, axis=1),
        'wout0': l0_w_out.astype(BF16),
        'win1': l1_w_in.astype(BF16), 'rnorm': ret_norm[None], 'wout1': l1_w_out.astype(BF16),
        'nf': norm_ffn[:, None], 'wg': ffn_w_gate.astype(BF16), 'wu': ffn_w_up.astype(BF16),
        'ffn_cw': ffn_conv_w, 'ffn_cb': ffn_conv_b[:, None], 'wd': ffn_w_down.astype(BF16),
        'npl': norm_ple[:, None], 'wpg': ple_w_gate.astype(BF16), 'wpp': ple_w_proj.astype(BF16),
        'nfin': norm_final[None],
    }


def kernel(x_prompt, x_sample, state_ssm_conv, state_ssm, state_ret, state_ffn_conv, p_prompt, p_sample, norm_mix, norm_ffn, norm_ple, norm_final, l0_w_in, gm_v_gain, gm_w_s, gm_b_s, ssm_conv_w, ssm_conv_b, ssm_dt_bias, ssm_a_log, ssm_d, ssm_norm, l0_w_out, l1_w_in, ret_norm, l1_w_out, ffn_w_gate, ffn_w_up, ffn_conv_w, ffn_conv_b, ffn_w_down, ple_w_gate, ple_w_proj):
    w = _prepare_weights(norm_mix, norm_ffn, norm_ple, norm_final, l0_w_in, gm_v_gain, gm_w_s, gm_b_s,
                         ssm_conv_w, ssm_conv_b, ssm_dt_bias, ssm_a_log, ssm_d, ssm_norm, l0_w_out,
                         l1_w_in, ret_norm, l1_w_out, ffn_w_gate, ffn_w_up, ffn_conv_w, ffn_conv_b,
                         ffn_w_down, ple_w_gate, ple_w_proj)
    y_p, _, conv_p, ssm_p, ret_p, ffn_p = _trunk(
        x_prompt, p_prompt, None, None, None, None, 0, w, ns=1, ns_ret=1, tl=256, tl_ssd=512, tl_ffn=1024,
        emit_gmv=False, skew_ret=True)
    y_s, gmv_s, conv_s, ssm_s, ret_s, ffn_s = _trunk(
        x_sample, p_sample, state_ssm_conv, state_ssm, state_ret, state_ffn_conv, PAST_LEN, w,
        ns=8, ns_ret=4, tl=32, tl_ssd=32, tl_ffn=32, emit_gmv=True, skew_ret=False)
    return (y_p, y_s, conv_p, conv_s, ssm_p, ssm_s, ret_p, ret_s, ffn_p, ffn_s, gmv_s)
```

```python
import functools

import jax
import jax.numpy as jnp
import numpy as np
from jax import lax
from jax.experimental import pallas as pl
from jax.experimental.pallas import tpu as pltpu

F32 = jnp.float32
BF16 = jnp.bfloat16

D_MODEL = 1024
DEPTH = 2
PAST_LEN = 1024
CHUNK = 64
PLE_DIM = 256
EPS = 1e-6
GM_CHUNK = 128
GM_GROUPS = 8
GM_GDIM = 128
SSM_HEADS = 16
SSM_HEAD_DIM = 64
SSM_DIM = 1024
SSM_GROUPS = 2
SSM_STATE = 128
SSM_CONV = 4
SSM_CONV_DIM = 1536
RET_HEADS = 8
RET_DK = 128
RET_DV = 256
RET_QK = 1024
RET_V = 2048
ROPE_BASE = 10000.0
D_FF = 2816
FFN_CONV = 3

SUBLANES = 8
LANES = 128
GROUP_LANES = SSM_HEADS // SSM_GROUPS * SSM_HEAD_DIM
DT_LANES = 128
VMEM_LIMIT = 62 * 1024 * 1024


def _dot(a, b):
    return jnp.dot(a, b, preferred_element_type=F32)


def _dot_nt(a, b):
    return lax.dot_general(a, b, (((1,), (1,)), ((), ())), preferred_element_type=F32)


def _dot_tn(a, b):
    return lax.dot_general(a, b, (((0,), (0,)), ((), ())), preferred_element_type=F32)


def _rms(x, g):
    return x * lax.rsqrt(jnp.mean(x * x, axis=-1, keepdims=True) + EPS) * g


def _silu(x):
    return x * jax.nn.sigmoid(x)


def _softplus(x):
    return jnp.maximum(x, 0.0) + jnp.log1p(jnp.exp(-jnp.abs(x)))


def _pad_store(pad_ref, start, value):
    rows = value.shape[1]
    for j in range(pad_ref.shape[1]):
        pad_ref[:, j, start:start + rows, :] = value[:, :, j * LANES:(j + 1) * LANES]


def _pad_load(pad_ref, start, rows, j0=0, j1=None):
    j1 = pad_ref.shape[1] if j1 is None else j1
    return jnp.concatenate([pad_ref[:, j, start:start + rows, :] for j in range(j0, j1)], axis=-1)


def _causal_conv(pad_ref, w_ref, b_ref, tl, j0=0, j1=None):
    j1 = pad_ref.shape[1] if j1 is None else j1
    width = w_ref.shape[0]
    lanes = slice(j0 * LANES, j1 * LANES)
    y = b_ref[:, lanes][None]
    for k in range(width):
        y = y + w_ref[k:k + 1, lanes][None] * _pad_load(pad_ref, SUBLANES - (width - 1) + k, tl, j0, j1)
    return y


class _Cols:
    def __init__(self, src):
        self.src = src

    def cols(self, a, b):
        return self.src[:, a:b]


PROJ_BLOCK = 512
LATE_BLOCK = 512
JOINT_ROWS = 128
BAND_ROWS = 256


def _proj_spec(outputs):
    spec = []
    for o, (w_ref, first, n_cols, _) in enumerate(outputs):
        for c in range(0, n_cols, PROJ_BLOCK):
            spec.append((o, w_ref, first + c, c, min(PROJ_BLOCK, n_cols - c)))
    return spec


def _skewed(t, norm_fn, outputs, body_fn, init_fn, x_refs, bufs, nl, final_fn=None):
    spec = _proj_spec(outputs)
    if bufs is None:
        (x_ref,) = x_refs

        @pl.when(t % nl == 0)
        def _():
            init_fn()

        x = x_ref[...].reshape(-1, D_MODEL)
        h = norm_fn(x)
        parts = [[] for _ in outputs]
        for o, w_ref, wc, _, n in spec:
            parts[o].append(_dot(h, w_ref[:, wc:wc + n]).astype(outputs[o][3]))
        body_fn(x, tuple(_Cols(jnp.concatenate(p, axis=1) if len(p) > 1 else p[0]) for p in parts), lambda n=1: None,
                h)
        if final_fn is not None:
            @pl.when(t % nl == nl - 1)
            def _():
                final_fn()
        return

    xa_ref, xb_ref = x_refs

    @pl.when(t == 0)
    def _():
        for r in bufs[1]:
            r[...] = jnp.zeros(r.shape, r.dtype)

    @pl.when(jnp.maximum(t - 1, 0) % nl == 0)
    def _():
        init_fn()

    for parity in range(2):
        @pl.when(t % 2 == parity)
        def _(parity=parity):
            h = norm_fn(xa_ref[...].reshape(-1, D_MODEL))
            pieces = iter(spec)

            def tick(n=1):
                for _ in range(n):
                    piece = next(pieces, None)
                    if piece is not None:
                        o, w_ref, wc, oc, width = piece
                        bufs[parity][o][:, oc:oc + width] = _dot(h, w_ref[:, wc:wc + width]).astype(outputs[o][3])

            body_fn(xb_ref[...].reshape(-1, D_MODEL), tuple(_Cols(r) for r in bufs[1 - parity]), tick, None)
            tick(len(spec))

    if final_fn is not None:
        @pl.when(jnp.maximum(t - 1, 0) % nl == nl - 1)
        def _():
            final_fn()


def _split_refs(refs, n_x, n_in, n_out, n_scratch):
    refs = list(refs)
    x_refs, refs = refs[:n_x], refs[n_x:]
    ins, refs = refs[:n_in], refs[n_in:]
    outs, refs = refs[:n_out], refs[n_out:]
    scratch, refs = refs[:n_scratch], refs[n_scratch:]
    bufs = (refs[:len(refs) // 2], refs[len(refs) // 2:]) if refs else None
    return x_refs, ins, outs, scratch, bufs


def _mixer0_kernel(*refs, ns, tl, nl, tc, n_gm, emit_gmv, skew, has_state):
    n_state = 2 if has_state else 0
    x_refs, ins, outs, scratch, bufs = _split_refs(refs, 2 if skew else 1, n_state + 15, 4 if emit_gmv else 3, 6)
    sconv_ref, sssm_ref = ins[:n_state] if has_state else (None, None)
    (nm_ref, wuvz_ref, wxbc_ref, wdt_ref, dtb_ref, alog_ref, expd_ref, dskip_ref, cw_ref,
     cb_ref, snorm_ref, vgain_ref, gmw_ref, gmb_ref, wout_ref) = ins[n_state:]
    xo_ref, convo_ref, sso_ref = outs[:3]
    gmv_ref = outs[3] if emit_gmv else None
    xpad_ref, xs_ref, cs_ref, y_ref, mix_ref, ht_ref = scratch
    pairs_per_group = SSM_HEADS // SSM_GROUPS // 2
    m = ns * tl
    nck = tl // tc
    nchunks = m // tc
    gm = GM_GROUPS * GM_GDIM

    def init():
        if not has_state:
            for j in range(xpad_ref.shape[1]):
                xpad_ref[:, j, 0:SUBLANES, :] = jnp.zeros((ns, SUBLANES, LANES), F32)
            ht_ref[...] = jnp.zeros(ht_ref.shape, F32)
            return
        _pad_store(xpad_ref, 0, sconv_ref[...])
        for s in range(ns):
            for pr in range(SSM_HEADS // 2):
                g, q = divmod(pr, pairs_per_group)
                ht_ref[s, g, :, q * 2 * SSM_HEAD_DIM:(q + 1) * 2 * SSM_HEAD_DIM] = sssm_ref[s, pr].T

    def final():
        for s in range(ns):
            for pr in range(SSM_HEADS // 2):
                g, q = divmod(pr, pairs_per_group)
                sso_ref[s, pr] = ht_ref[s, g, :, q * 2 * SSM_HEAD_DIM:(q + 1) * 2 * SSM_HEAD_DIM].T

    def norm(x):
        return _rms(x, nm_ref[...]).astype(BF16)

    assert not skew
    outputs = [(wuvz_ref, 0, 2 * gm, F32), (wxbc_ref, 0, SSM_CONV_DIM, F32), (wdt_ref, 0, DT_LANES, F32)]

    def body(x, projected, tick, h):
        uvz, xbc, dtr = projected

        z_parts, xg_parts = {}, {}

        def z_block(c0):
            z_parts[c0] = _dot(h, wuvz_ref[:, 2 * gm + c0:2 * gm + c0 + LATE_BLOCK])

        def out_gm_block(c0):
            xg_parts[c0] = x[:, c0:c0 + LATE_BLOCK] + _dot(mix_ref[:, :gm], wout_ref[:gm, c0:c0 + LATE_BLOCK])

        late = [functools.partial(f, c0) for c0 in range(0, SSM_DIM, LATE_BLOCK) for f in (out_gm_block, z_block)]

        def emit_late():
            if late:
                late.pop(0)()

        ri = lax.broadcasted_iota(jnp.int32, (n_gm, n_gm), 0)
        ci = lax.broadcasted_iota(jnp.int32, (n_gm, n_gm), 1)
        gm_mask = (ci // CHUNK) <= (ri // CHUNK)
        for g in range(GM_GROUPS):
            lanes = slice(g * GM_GDIM, (g + 1) * GM_GDIM)
            u_g = jax.nn.gelu(uvz.cols(g * GM_GDIM, (g + 1) * GM_GDIM))
            v_g = jax.nn.gelu(uvz.cols(gm + g * GM_GDIM, gm + (g + 1) * GM_GDIM))
            v_g = v_g * lax.rsqrt(jnp.mean(v_g * v_g, axis=-1, keepdims=True) + EPS) * vgain_ref[:, lanes]
            if emit_gmv:
                gmv_ref[:, :, lanes] = v_g.reshape(ns, tl, GM_GDIM)
            w_g = jnp.where(gm_mask, gmw_ref[g], 0.0).astype(BF16)
            v_gb = v_g.astype(BF16)
            for c in range(m // n_gm):
                rows = slice(c * n_gm, (c + 1) * n_gm)
                s = _dot(w_g, v_gb[rows]) + gmb_ref[:, lanes]
                mix_ref[rows, lanes] = (u_g[rows] * s).astype(BF16)

        for j in range(SSM_CONV_DIM // LANES):
            xpad_ref[:, j, SUBLANES:, :] = xbc.cols(j * LANES, (j + 1) * LANES).reshape(ns, tl, LANES)
        emit_late()
        xc = _silu(_causal_conv(xpad_ref, cw_ref, cb_ref, tl)).reshape(m, SSM_CONV_DIM)
        tail = _pad_load(xpad_ref, tl, SUBLANES)
        _pad_store(xpad_ref, 0, tail)
        convo_ref[...] = tail
        emit_late()
        xh = xc[:, :SSM_DIM]
        bm = xc[:, SSM_DIM:SSM_DIM + SSM_GROUPS * SSM_STATE].astype(BF16)
        cm = xc[:, SSM_DIM + SSM_GROUPS * SSM_STATE:].astype(BF16)

        dt_n = _softplus(dtr.cols(0, DT_LANES) + dtb_ref[...])
        da_n = dt_n * (-jnp.exp(alog_ref[...]))
        emit_late()

        def split3(v):
            hi = v.astype(BF16).astype(F32)
            r1 = v - hi
            mid = r1.astype(BF16).astype(F32)
            return hi, mid, r1 - mid

        lane_grp = lax.broadcasted_iota(jnp.int32, (m, DT_LANES), 1) // SSM_HEADS

        def expand(v):
            hi, mid, lo = split3(v)
            parts = jnp.where(lane_grp == 0, hi, jnp.where(lane_grp == 1, mid, lo))
            return _dot(parts.astype(BF16), expd_ref[:, :SSM_DIM])

        band = min(m, BAND_ROWS)
        ri = lax.broadcasted_iota(jnp.int32, (band, band), 0)
        ci = lax.broadcasted_iota(jnp.int32, (band, band), 1)
        tri = jnp.where(((ri // tc) == (ci // tc)) & (ci <= ri), 1.0, 0.0).astype(BF16)
        da3 = jnp.concatenate(split3(da_n), axis=1).astype(BF16)
        cs3 = [_dot(tri, da3[b * band:(b + 1) * band]) for b in range(m // band)]
        cs3 = jnp.concatenate(cs3, axis=0) if len(cs3) > 1 else cs3[0]
        cs_n = cs3[:, :DT_LANES] + cs3[:, DT_LANES:2 * DT_LANES] + cs3[:, 2 * DT_LANES:]
        cs_ref[...] = expand(cs_n)
        dtx = expand(dt_n)
        xs_ref[...] = xh * dtx
        y_ref[...] = dskip_ref[...] * xh
        emit_late()

        cpb = band // tc

        def dup_rows(b_g):
            parts = []
            for k in range(cpb):
                blk = b_g[k * tc:(k + 1) * tc]
                if tc < CHUNK:
                    blk = jnp.concatenate([blk, jnp.zeros((CHUNK - tc, SSM_STATE), BF16)], axis=0)
                parts += [blk, blk]
            return jnp.concatenate(parts, axis=0)

        cb2 = [[_dot_nt(cm[b * band:(b + 1) * band, g * SSM_STATE:(g + 1) * SSM_STATE],
                        dup_rows(bm[b * band:(b + 1) * band, g * SSM_STATE:(g + 1) * SSM_STATE]))
                for g in range(SSM_GROUPS)] for b in range(m // band)]

        t_i = lax.broadcasted_iota(jnp.int32, (tc, SSM_DIM), 0)
        s_i = lax.broadcasted_iota(jnp.int32, (tc, SSM_DIM), 1) % SSM_HEAD_DIM
        causal = s_i <= t_i
        diag = s_i == t_i
        quad_head = lax.broadcasted_iota(jnp.int32, (tc, 4 * SSM_HEAD_DIM), 1) // SSM_HEAD_DIM
        heads_per_group = SSM_HEADS // SSM_GROUPS

        for c in range(nchunks):
            seq = c // nck
            rows = slice(c * tc, (c + 1) * tc)
            cs_c = cs_ref[rows, :]
            cs_row = jnp.sum(jnp.where(diag, cs_c, 0.0), axis=0, keepdims=True)
            cs_last = cs_c[tc - 1:tc, :]
            lmat = jnp.exp(jnp.where(causal, cs_c - cs_row, -1e30))
            b, k = divmod(c, cpb)
            cb_c = jnp.concatenate(
                [cb2[b][g][k * tc:(k + 1) * tc, k * 2 * CHUNK:(k + 1) * 2 * CHUNK] for g in range(SSM_GROUPS)
                 for _ in range(heads_per_group // 2)], axis=1)
            mm = (lmat * cb_c).astype(BF16)
            xs_c = xs_ref[rows, :]
            y_parts = []
            for q in range(SSM_HEADS // 4):
                ql = slice(q * 4 * SSM_HEAD_DIM, (q + 1) * 4 * SSM_HEAD_DIM)
                xq = xs_c[:, ql]
                blocks = []
                for hh in range(4):
                    blk = jnp.where(quad_head == hh, xq, 0.0).astype(BF16)
                    if tc < CHUNK:
                        blk = jnp.concatenate([blk, jnp.zeros((CHUNK - tc, 4 * SSM_HEAD_DIM), BF16)], axis=0)
                    blocks.append(blk)
                y_parts.append(_dot(mm[:, ql], jnp.concatenate(blocks, axis=0)))
            y_c = jnp.concatenate(y_parts, axis=1)
            ecs = jnp.exp(cs_c)
            xsd = (xs_c * jnp.exp(cs_last - cs_c)).astype(BF16)
            e_last = ecs[tc - 1:tc, :]
            y_off = []
            for g in range(SSM_GROUPS):
                gl = slice(g * GROUP_LANES, (g + 1) * GROUP_LANES)
                sl = slice(g * SSM_STATE, (g + 1) * SSM_STATE)
                h_prev = ht_ref[seq, g]
                y_off.append(_dot(cm[rows, sl], h_prev.astype(BF16)))
                ht_ref[seq, g] = h_prev * e_last[:, gl] + _dot_tn(bm[rows, sl], xsd[:, gl])
            y_ref[rows, :] = y_ref[rows, :] + y_c + jnp.concatenate(y_off, axis=1) * ecs
        while late:
            emit_late()

        z = jnp.concatenate([z_parts[c0] for c0 in sorted(z_parts)], axis=1)
        x_gm = jnp.concatenate([xg_parts[c0] for c0 in sorted(xg_parts)], axis=1)
        yb = _rms(y_ref[...] * _silu(z), snorm_ref[...])
        xo_ref[...] = (x_gm + _dot(yb.astype(BF16), wout_ref[gm:, :D_MODEL])).reshape(ns, tl, D_MODEL)

    _skewed(pl.program_id(0), norm, outputs, body, init, x_refs, bufs, nl, final)


def _mixer1_kernel(*refs, ns, tl, nl, skew, has_state):
    n_state = 1 if has_state else 0
    x_refs, ins, outs, scratch, bufs = _split_refs(refs, 2 if skew else 1, n_state + 10, 2, 1)
    sret_ref = ins[0] if has_state else None
    (nm_ref, win_ref, cos_ref, sin_ref, dmat_ref, qdec_ref, kdec_ref, cdec_ref, rnorm_ref,
     wout_ref) = ins[n_state:]
    xo_ref, s_ref = outs
    (mix_ref,) = scratch
    m = ns * tl

    def init():
        s_ref[...] = sret_ref[...] if has_state else jnp.zeros(s_ref.shape, F32)

    def norm(x):
        return _rms(x, nm_ref[...]).astype(BF16)

    outputs = [(win_ref, 0, 2 * RET_QK, F32), (win_ref, 2 * RET_QK, RET_V, BF16), (win_ref, 2 * RET_QK + RET_V, RET_V, F32)]
    n_ticks = RET_HEADS + RET_HEADS // 2
    per_tick = -(-len(_proj_spec(outputs)) // n_ticks)

    def body(x, projected, tick, _h):
        qk, vv, gg = projected
        cos = jnp.concatenate([cos_ref[...]] * ns, axis=0) if ns > 1 else cos_ref[...]
        sin = jnp.concatenate([sin_ref[...]] * ns, axis=0) if ns > 1 else sin_ref[...]
        nseg = m // dmat_ref.shape[1]
        seg = m // nseg
        qdec = jnp.concatenate([qdec_ref[...]] * nseg, axis=0) if nseg > 1 else qdec_ref[...]
        kdec = jnp.concatenate([kdec_ref[...]] * nseg, axis=0) if nseg > 1 else kdec_ref[...]
        scale = RET_DK ** -0.5
        for hd in range(RET_HEADS):
            kl = slice(hd * RET_DK, (hd + 1) * RET_DK)
            vl = slice(hd * RET_DV, (hd + 1) * RET_DV)
            q = qk.cols(hd * RET_DK, (hd + 1) * RET_DK)
            k = qk.cols(RET_QK + hd * RET_DK, RET_QK + (hd + 1) * RET_DK)
            v = vv.cols(hd * RET_DV, (hd + 1) * RET_DV)
            gate = gg.cols(hd * RET_DV, (hd + 1) * RET_DV)
            q = q * cos + pltpu.roll(q, RET_DK // 2, axis=1) * sin
            k = (k * cos + pltpu.roll(k, RET_DK // 2, axis=1) * sin) * scale
            qb, kb = q.astype(BF16), k.astype(BF16)
            y_intra = []
            for sg in range(nseg):
                rows = slice(sg * seg, (sg + 1) * seg)
                sc = _dot_nt(qb[rows], kb[rows]) * dmat_ref[hd]
                y_intra.append(_dot(sc.astype(BF16), v[rows]))
            y = jnp.concatenate(y_intra, axis=0) if nseg > 1 else y_intra[0]
            qd = (q * qdec[:, kl]).astype(BF16)
            kd = (k * kdec[:, kl]).astype(BF16)
            y_inter = []
            for s in range(ns):
                rows = slice(s * tl, (s + 1) * tl)
                s_prev = s_ref[s, hd]
                y_inter.append(_dot(qd[rows], s_prev.astype(BF16)))
                s_ref[s, hd] = s_prev * cdec_ref[:, vl] + _dot_tn(kd[rows], v[rows])
            y = y + (jnp.concatenate(y_inter, axis=0) if ns > 1 else y_inter[0])
            if hd % 2 == 0:
                tick(per_tick)
            mu = jnp.mean(y, axis=-1, keepdims=True)
            yc = y - mu
            var = jnp.mean(yc * yc, axis=-1, keepdims=True)
            yn = yc * lax.rsqrt(var + EPS) * rnorm_ref[:, vl]
            mix_ref[:, vl] = (_silu(gate) * yn).astype(BF16)
            tick(per_tick)
            if hd % 4 == 3:
                pair = slice((hd - 3) * RET_DV, (hd + 1) * RET_DV)
                x = x + _dot(mix_ref[:, pair], wout_ref[pair, :D_MODEL])
        xo_ref[...] = x.reshape(ns, tl, D_MODEL)

    _skewed(pl.program_id(0), norm, outputs, body, init, x_refs, bufs, nl)


def _ffn_kernel(*refs, ns, tl, nl, final_norm, skew, has_state):
    n_state = 1 if has_state else 0
    x_refs, ins, outs, scratch, bufs = _split_refs(refs, 2 if skew else 1, n_state + 11, 2, 1)
    p_ref = ins[0]
    sbuf_ref = ins[1] if has_state else None
    (nf_ref, wg_ref, wu_ref, cw_ref, cb_ref, wd_ref, npl_ref, wpg_ref, wpp_ref, nfin_ref) = ins[1 + n_state:]
    xo_ref, bufo_ref = outs
    (gpad_ref,) = scratch
    m = ns * tl

    def init():
        if has_state:
            _pad_store(gpad_ref, 0, sbuf_ref[...])
        else:
            for j in range(gpad_ref.shape[1]):
                gpad_ref[:, j, 0:SUBLANES, :] = jnp.zeros((ns, SUBLANES, LANES), F32)

    def norm(x):
        return _rms(x, nf_ref[...]).astype(BF16)

    outputs = []

    def body(x, projected, tick, h):
        pp = _dot(p_ref[...].reshape(m, PLE_DIM).astype(BF16), wpp_ref[:, :D_MODEL])
        _pad_store(gpad_ref, SUBLANES, _dot(h, wg_ref[...]).reshape(ns, tl, D_FF))
        gate = _causal_conv(gpad_ref, cw_ref, cb_ref, tl).reshape(m, D_FF)
        tail = _pad_load(gpad_ref, tl, SUBLANES)
        _pad_store(gpad_ref, 0, tail)
        bufo_ref[...] = tail
        act = (_silu(gate) * _dot(h, wu_ref[...])).astype(BF16)
        x = x + _dot(act, wd_ref[:, :D_MODEL])
        pgate = jax.nn.sigmoid(_dot(_rms(x, npl_ref[...]).astype(BF16), wpg_ref[:, :D_MODEL]))
        x = x + pgate * pp
        if final_norm:
            x = _rms(x, nfin_ref[...])
        xo_ref[...] = x.reshape(ns, tl, D_MODEL)

    _skewed(pl.program_id(0), norm, outputs, body, init, x_refs, bufs, nl)


class _Tiling:
    def __init__(self, bsz, length, ns, tl, skew):
        self.ns, self.tl, self.skew = ns, tl, skew
        self.nl = length // tl
        self.n_tiles = (bsz // ns) * self.nl
        self.grid = (self.n_tiles + 1,) if skew else (self.n_tiles,)

    def body_tile(self, t):
        return jnp.maximum(t - 1, 0) if self.skew else t

    def proj_tile(self, t):
        return jnp.minimum(t, self.n_tiles - 1)

    def x_specs(self, width):
        blk = (self.ns, self.tl, width)
        body = pl.BlockSpec(blk, lambda t: (self.body_tile(t) // self.nl, self.body_tile(t) % self.nl, 0))
        if not self.skew:
            return [body]
        return [pl.BlockSpec(blk, lambda t: (self.proj_tile(t) // self.nl, self.proj_tile(t) % self.nl, 0)), body]

    def tile(self, width):
        return self.x_specs(width)[-1]

    def state(self, tail):
        zeros = (0,) * len(tail)
        return pl.BlockSpec((self.ns,) + tail, lambda t: (self.body_tile(t) // self.nl,) + zeros)

    @staticmethod
    def _window(arr, shape):
        if arr.dtype == BF16 and len(shape) >= 2 and (shape[-1] // LANES) % SUBLANES == 0:
            return shape[:-1] + (shape[-1] + LANES,)
        return shape

    def const(self, arr):
        zeros = (0,) * arr.ndim
        return pl.BlockSpec(self._window(arr, arr.shape), lambda t: zeros, pipeline_mode=pl.Buffered(1))

    def layer(self, arr, layer):
        index = (layer,) + (0,) * (arr.ndim - 1)
        return pl.BlockSpec((None,) + self._window(arr, arr.shape[1:]), lambda t: index,
                            pipeline_mode=pl.Buffered(1))

    def bufs(self, shapes):
        return [pltpu.VMEM(s, d) for _ in range(2) for s, d in shapes] if self.skew else []


def _params():
    return pltpu.CompilerParams(dimension_semantics=("arbitrary",), vmem_limit_bytes=VMEM_LIMIT)


def _mixer0(x, sconv, sssm, w, *, ns, tl, emit_gmv, skew):
    bsz, length, _ = x.shape
    tg = _Tiling(bsz, length, ns, tl, skew)
    tc = min(tl, CHUNK)
    n_gm = min(tl, GM_CHUNK)
    m = ns * tl
    has_state = sconv is not None
    pair_shape = (SSM_HEADS // 2, 2 * SSM_HEAD_DIM, SSM_STATE)
    states = [_header(sconv), sssm.reshape((bsz,) + pair_shape)] if has_state else []
    state_specs = [tg.state((SUBLANES, SSM_CONV_DIM)), tg.state(pair_shape)]
    uvz_cols = 2 * GM_GROUPS * GM_GDIM + SSM_DIM
    assert uvz_cols % SSM_CONV_DIM == 0 and uvz_cols + LANES <= w['win0'].shape[1]
    win_specs = [tg.const(w['nm0']),
                 pl.BlockSpec((D_MODEL, uvz_cols + LANES), lambda t: (0, 0), pipeline_mode=pl.Buffered(1)),
                 pl.BlockSpec((D_MODEL, SSM_CONV_DIM), lambda t: (0, uvz_cols // SSM_CONV_DIM),
                              pipeline_mode=pl.Buffered(1))]
    consts = [w['nm0'], w['win0'], w['win0'], w['wdt'], w['dtb'], w['alog'], w['expd'], w['dskip'], w['ssm_cw'],
              w['ssm_cb'], w['snorm'], w['vgain'], w['gmw'][:, :n_gm, :n_gm], w['gmb'][:n_gm], w['wout0']]
    out_shape = [jax.ShapeDtypeStruct(x.shape, F32),
                 jax.ShapeDtypeStruct((bsz, SUBLANES, SSM_CONV_DIM), F32),
                 jax.ShapeDtypeStruct((bsz,) + pair_shape, F32)]
    out_specs = [tg.tile(D_MODEL)] + state_specs
    if emit_gmv:
        out_shape.append(jax.ShapeDtypeStruct((bsz, length, GM_GROUPS * GM_GDIM), F32))
        out_specs.append(tg.tile(GM_GROUPS * GM_GDIM))
    outs = pl.pallas_call(
        functools.partial(_mixer0_kernel, ns=ns, tl=tl, nl=tg.nl, tc=tc, n_gm=n_gm, emit_gmv=emit_gmv, skew=skew,
                          has_state=has_state),
        grid=tg.grid,
        in_specs=tg.x_specs(D_MODEL) + state_specs[:len(states)] + win_specs
        + [tg.const(c) for c in consts[len(win_specs):]],
        out_specs=out_specs,
        out_shape=out_shape,
        scratch_shapes=[pltpu.VMEM((ns, SSM_CONV_DIM // LANES, SUBLANES + tl, LANES), F32),
                        pltpu.VMEM((m, SSM_DIM), F32), pltpu.VMEM((m, SSM_DIM), F32),
                        pltpu.VMEM((m, SSM_DIM), F32), pltpu.VMEM((m, GM_GROUPS * GM_GDIM), BF16),
                        pltpu.VMEM((ns, SSM_GROUPS, SSM_STATE, GROUP_LANES), F32)]
        + tg.bufs([((m, 2 * GM_GROUPS * GM_GDIM + SSM_DIM), F32), ((m, SSM_CONV_DIM), F32), ((m, DT_LANES), F32)]),
        compiler_params=_params(),
        name="mixer0",
    )(*([x] * len(tg.x_specs(D_MODEL))), *states, *consts)
    conv_o = outs[1][:, SUBLANES - (SSM_CONV - 1):]
    ssm_o = outs[2].reshape(bsz, SSM_HEADS, SSM_HEAD_DIM, SSM_STATE)
    return outs[0], conv_o, ssm_o, (outs[3] if emit_gmv else None)


def _mixer1(x, sret, w, pos0, *, ns, tl, skew):
    bsz, length, _ = x.shape
    tg = _Tiling(bsz, length, ns, tl, skew)
    m = ns * tl
    tabs = _retention_tables(ns if tl < JOINT_ROWS else 1, tl, pos0, length)
    has_state = sret is not None
    state_shape = (RET_HEADS, RET_DK, RET_DV)
    states = [sret] if has_state else []
    consts_a = [w['nm1'], w['win1']]
    consts_b = [tabs['dmat'], tabs['qdec'], tabs['kdec'], tabs['cdec'], w['rnorm'], w['wout1']]
    pos_spec = pl.BlockSpec((tl, RET_DK), lambda t: (tg.body_tile(t) % tg.nl, 0))
    return pl.pallas_call(
        functools.partial(_mixer1_kernel, ns=ns, tl=tl, nl=tg.nl, skew=skew, has_state=has_state),
        grid=tg.grid,
        in_specs=tg.x_specs(D_MODEL) + [tg.state(state_shape)] * len(states)
        + [tg.const(c) for c in consts_a] + [pos_spec, pos_spec]
        + [tg.const(c) for c in consts_b],
        out_specs=[tg.tile(D_MODEL), tg.state(state_shape)],
        out_shape=[jax.ShapeDtypeStruct(x.shape, F32), jax.ShapeDtypeStruct((bsz,) + state_shape, F32)],
        scratch_shapes=[pltpu.VMEM((m, RET_V), BF16)]
        + tg.bufs([((m, 2 * RET_QK), F32), ((m, RET_V), BF16), ((m, RET_V), F32)]),
        compiler_params=_params(),
        name="mixer1",
    )(*([x] * len(tg.x_specs(D_MODEL))), *states, *consts_a, tabs['cos'], tabs['sin'], *consts_b)


def _ffn(x, p, sbuf, w, layer, *, ns, tl, final_norm):
    bsz, length, _ = x.shape
    tg = _Tiling(bsz, length, ns, tl, False)
    has_state = sbuf is not None
    states = [_header(sbuf)] if has_state else []
    stacked = [w['nf'], w['wg'], w['wu'], w['ffn_cw'], w['ffn_cb'], w['wd'], w['npl'], w['wpg'], w['wpp']]
    p_spec = pl.BlockSpec((None, ns, tl, PLE_DIM),
                          lambda t: (layer, tg.body_tile(t) // tg.nl, tg.body_tile(t) % tg.nl, 0))
    x, buf_o = pl.pallas_call(
        functools.partial(_ffn_kernel, ns=ns, tl=tl, nl=tg.nl, final_norm=final_norm, skew=False,
                          has_state=has_state),
        grid=tg.grid,
        in_specs=tg.x_specs(D_MODEL) + [p_spec] + [tg.state((SUBLANES, D_FF))] * len(states)
        + [tg.layer(c, layer) for c in stacked] + [tg.const(w['nfin'])],
        out_specs=[tg.tile(D_MODEL), tg.state((SUBLANES, D_FF))],
        out_shape=[jax.ShapeDtypeStruct(x.shape, F32), jax.ShapeDtypeStruct((bsz, SUBLANES, D_FF), F32)],
        scratch_shapes=[pltpu.VMEM((ns, D_FF // LANES, SUBLANES + tl, LANES), F32)],
        compiler_params=_params(),
        name="ffn%d" % layer,
    )(x, p, *states, *stacked, w['nfin'])
    return x, buf_o[:, SUBLANES - (FFN_CONV - 1):]


def _retention_tables(ns, tl, pos0, length):
    half = RET_DK // 2
    inv = ROPE_BASE ** (-np.arange(half, dtype=np.float64) / half)
    ang = (pos0 + np.arange(length, dtype=np.float64))[:, None] * inv[None]
    cos, sin = np.cos(ang), np.sin(ang)
    lg = np.log(1.0 - 2.0 ** (-5.0 - np.arange(RET_HEADS, dtype=np.float64)))
    m = ns * tl
    t = np.arange(m) % tl
    same_seq = (np.arange(m)[:, None] // tl) == (np.arange(m)[None, :] // tl)
    diff = (t[:, None] - t[None, :]).astype(np.float64)
    dmat = np.where((diff[None] >= 0) & same_seq[None], np.exp(lg[:, None, None] * np.maximum(diff, 0.0)[None]), 0.0)
    qdec = np.exp(lg[None, :] * (t[:, None] + 1.0))
    kdec = np.exp(lg[None, :] * (tl - 1.0 - t)[:, None])
    cdec = np.exp(lg * tl)
    tabs = {
        'cos': np.concatenate([cos, cos], axis=-1),
        'sin': np.concatenate([-sin, sin], axis=-1),
        'dmat': dmat,
        'qdec': np.repeat(qdec, RET_DK, axis=1),
        'kdec': np.repeat(kdec, RET_DK, axis=1),
        'cdec': np.repeat(cdec, RET_DV)[None],
    }
    return {k: jnp.asarray(v, F32) for k, v in tabs.items()}


def _header(buf):
    return jnp.pad(buf, ((0, 0), (SUBLANES - buf.shape[1], 0), (0, 0)))


def _trunk(x, p, sconv, sssm, sret, sffn, pos0, w, *, ns, ns_ret, tl, tl_ssd, tl_ffn, emit_gmv, skew_ret):
    x, conv_o, ssm_o, gmv = _mixer0(x, sconv, sssm, w, ns=ns, tl=tl_ssd, emit_gmv=emit_gmv, skew=False)
    x, ffn0 = _ffn(x, p, None if sffn is None else sffn[0], w, 0, ns=ns, tl=tl_ffn, final_norm=False)
    x, ret_o = _mixer1(x, sret, w, pos0, ns=ns_ret, tl=tl, skew=skew_ret)
    x, ffn1 = _ffn(x, p, None if sffn is None else sffn[1], w, 1, ns=ns, tl=tl_ffn, final_norm=True)
    return x, gmv, conv_o, ssm_o, ret_o, jnp.stack([ffn0, ffn1])


def _prepare_weights(norm_mix, norm_ffn, norm_ple, norm_final, l0_w_in, gm_v_gain, gm_w_s, gm_b_s,
                     ssm_conv_w, ssm_conv_b, ssm_dt_bias, ssm_a_log, ssm_d, ssm_norm, l0_w_out,
                     l1_w_in, ret_norm, l1_w_out, ffn_w_gate, ffn_w_up, ffn_conv_w, ffn_conv_b,
                     ffn_w_down, ple_w_gate, ple_w_proj):
    gm = GM_GROUPS * GM_GDIM
    rep = lambda v: jnp.repeat(v, SSM_HEAD_DIM)[None].astype(F32)
    tile = lambda v: jnp.tile(v, DT_LANES // SSM_HEADS)[None].astype(F32)
    lane = np.arange(DT_LANES)

    return {
        'nm0': norm_mix[0][None], 'nm1': norm_mix[1][None],
        'win0': l0_w_in.astype(BF16),
        'wdt': jnp.tile(l0_w_in[:, 2 * gm + SSM_DIM + SSM_CONV_DIM:], (1, DT_LANES // SSM_HEADS)).astype(BF16),
        'dtb': tile(ssm_dt_bias), 'alog': tile(ssm_a_log), 'dskip': rep(ssm_d),
        'expd': jnp.asarray((lane[:, None] < 3 * SSM_HEADS)
                            & (lane[:, None] % SSM_HEADS == np.arange(SSM_DIM)[None] // SSM_HEAD_DIM), BF16),
        'ssm_cw': ssm_conv_w, 'ssm_cb': ssm_conv_b[None], 'snorm': ssm_norm[None], 'vgain': gm_v_gain[None],
        'gmw': gm_w_s, 'gmb': jnp.repeat(gm_b_s.T, GM_GDIM, axis=1),
        'wout0': l0_w_out.astype(BF16),
        'win1': l1_w_in.astype(BF16), 'rnorm': ret_norm[None], 'wout1': l1_w_out.astype(BF16),
        'nf': norm_ffn[:, None], 'wg': ffn_w_gate.astype(BF16), 'wu': ffn_w_up.astype(BF16),
        'ffn_cw': ffn_conv_w, 'ffn_cb': ffn_conv_b[:, None], 'wd': ffn_w_down.astype(BF16),
        'npl': norm_ple[:, None], 'wpg': ple_w_gate.astype(BF16), 'wpp': ple_w_proj.astype(BF16),
        'nfin': norm_final[None],
    }


def kernel(x_prompt, x_sample, state_ssm_conv, state_ssm, state_ret, state_ffn_conv, p_prompt, p_sample, norm_mix, norm_ffn, norm_ple, norm_final, l0_w_in, gm_v_gain, gm_w_s, gm_b_s, ssm_conv_w, ssm_conv_b, ssm_dt_bias, ssm_a_log, ssm_d, ssm_norm, l0_w_out, l1_w_in, ret_norm, l1_w_out, ffn_w_gate, ffn_w_up, ffn_conv_w, ffn_conv_b, ffn_w_down, ple_w_gate, ple_w_proj):
    w = _prepare_weights(norm_mix, norm_ffn, norm_ple, norm_final, l0_w_in, gm_v_gain, gm_w_s, gm_b_s,
                         ssm_conv_w, ssm_conv_b, ssm_dt_bias, ssm_a_log, ssm_d, ssm_norm, l0_w_out,
                         l1_w_in, ret_norm, l1_w_out, ffn_w_gate, ffn_w_up, ffn_conv_w, ffn_conv_b,
                         ffn_w_down, ple_w_gate, ple_w_proj)
    y_p, _, conv_p, ssm_p, ret_p, ffn_p = _trunk(
        x_prompt, p_prompt, None, None, None, None, 0, w, ns=1, ns_ret=1, tl=256, tl_ssd=512, tl_ffn=1024,
        emit_gmv=False, skew_ret=True)
    y_s, gmv_s, conv_s, ssm_s, ret_s, ffn_s = _trunk(
        x_sample, p_sample, state_ssm_conv, state_ssm, state_ret, state_ffn_conv, PAST_LEN, w,
        ns=8, ns_ret=4, tl=32, tl_ssd=32, tl_ffn=32, emit_gmv=True, skew_ret=False)
    return (y_p, y_s, conv_p, conv_s, ssm_p, ssm_s, ret_p, ret_s, ffn_p, ffn_s, gmv_s)
```
